```python
import math
import jax, jax.numpy as jnp
from jax import lax
import numpy as np

D_MODEL = 1024
BATCH = 16
SEQ = 2048
DEPTH = 2
DEC_BATCH = 128
DEC_SEQ = 4
PAST_LEN = 16384
PAGE_SIZE = 128

HEAD_DIM = 64
H_MLA = 6
H_SB = 4
H_NSA = 6
MLA_Q_RANK = 256
MLA_KV_RANK = 256
MLA_NOPE = 64
MLA_ROPE = 32
MLA_V = 64
MLA_SCALE = (MLA_NOPE + MLA_ROPE) ** -0.5
ROPE_BASE = 10000.0
CMP_BLOCK = 32
CMP_STRIDE = 16
CMP_HIDDEN = HEAD_DIM
SEL_BLOCK = 64
SEL_TOP_N = 16
FORCE_SCORE = 1e4
WINDOW = 512
N_BUCKETS = 32
MAX_DISTANCE = 128
D_FF = -(-8 * D_MODEL // (3 * 256)) * 256
PLE_DIM = 256
QBLK = 128
EPS = 1e-6
NEG_BIG = -1e30
POOL_NUM = 5
POOL_DEN = 4
D_MIX = H_MLA * MLA_V + H_SB * HEAD_DIM + H_NSA * HEAD_DIM
IN_SPLITS = [MLA_Q_RANK, MLA_KV_RANK, MLA_ROPE, H_SB * HEAD_DIM, 2 * HEAD_DIM,
             H_NSA * HEAD_DIM, 4 * HEAD_DIM, 2 * HEAD_DIM, 3 * H_NSA]
D_IN = sum(IN_SPLITS)

kernel_name = 'hymba_mla_sb_nsa_step'


def rmsnorm(x, g):
    xf = x.astype(jnp.float32)
    xf = xf * lax.rsqrt(jnp.mean(xf * xf, axis=-1, keepdims=True) + EPS)
    return (xf * g.astype(jnp.float32)).astype(x.dtype)


def rope(x, pos):
    half = x.shape[-1] // 2
    freqs = ROPE_BASE ** (-jnp.arange(half, dtype=jnp.float32) / half)
    ang = pos.astype(jnp.float32)[:, None] * freqs[None, :]
    shape = (pos.shape[0],) + (1,) * (x.ndim - 3) + (half,)
    cos = jnp.cos(ang).reshape(shape)
    sin = jnp.sin(ang).reshape(shape)
    xf = x.astype(jnp.float32)
    x1, x2 = xf[..., :half], xf[..., half:]
    return jnp.concatenate([x1 * cos - x2 * sin, x1 * sin + x2 * cos], axis=-1).astype(x.dtype)


def masked_softmax(s, mask):
    s = jnp.where(mask, s.astype(jnp.float32), NEG_BIG)
    m = jnp.max(s, axis=-1, keepdims=True)
    e = jnp.where(mask, jnp.exp(s - m), 0.0)
    return e / jnp.maximum(jnp.sum(e, axis=-1, keepdims=True), 1e-30)


def t5_bucket(dist):
    n = jnp.maximum(dist, 0)
    max_exact = N_BUCKETS // 2
    nf = jnp.maximum(n, 1).astype(jnp.float32)
    large = max_exact + (jnp.log(nf / max_exact) / math.log(MAX_DISTANCE / max_exact)
                         * (N_BUCKETS - max_exact)).astype(jnp.int32)
    large = jnp.minimum(large, N_BUCKETS - 1)
    return jnp.where(n < max_exact, n, large)


def rel_bias_at(rel_bias, dist):
    return jnp.moveaxis(rel_bias[t5_bucket(dist)].astype(jnp.float32), -1, 1)


def mla_core(q_lat, q_pe, ckv, kpe, q_pos, k_pos):
    s = (jnp.einsum('qhc,kc->qhk', q_lat, ckv).astype(jnp.float32)
         + jnp.einsum('qhr,kr->qhk', q_pe, kpe).astype(jnp.float32)) * MLA_SCALE
    mask = (k_pos[None, :] <= q_pos[:, None])[:, None, :]
    p = masked_softmax(s, mask)
    return jnp.einsum('qhk,kc->qhc', p.astype(ckv.dtype), ckv)


def sb_core(q, k, v, q_pos, k_pos):
    z = jnp.einsum('qhd,kd->qhk', q, k).astype(jnp.float32) * (q.shape[-1] ** -0.5)
    mask = (k_pos[None, :] < q_pos[:, None])[:, None, :]
    log_keep = jnp.where(mask, jax.nn.log_sigmoid(-z), 0.0)
    incl = lax.cumsum(log_keep, axis=2, reverse=True)
    excl = jnp.concatenate([incl[..., 1:], jnp.zeros_like(incl[..., :1])], axis=-1)
    a = jnp.where(mask, jnp.exp(jax.nn.log_sigmoid(z) + excl), 0.0)
    return jnp.einsum('qhk,kd->qhd', a.astype(v.dtype), v)


def compress(rows, pe, w1, w2):
    n_sub = rows.shape[0] // CMP_STRIDE
    d = rows.shape[1]
    sub = rows[: n_sub * CMP_STRIDE].reshape(n_sub, CMP_STRIDE, d)
    blocks = jnp.concatenate([sub[:-1], sub[1:]], axis=1) + pe
    hid = jax.nn.gelu(blocks.reshape(n_sub - 1, CMP_BLOCK * d) @ w1)
    return hid @ w2


def to_blocks(rows):
    t, d = rows.shape
    n_sel = -(-t // SEL_BLOCK)
    rows = jnp.pad(rows, ((0, n_sel * SEL_BLOCK - t), (0, 0)))
    return rows.reshape(n_sel, SEL_BLOCK, d)


def nsa_prepare(nsa_rows, pe_k, w1_k, w2_k, pe_v, w1_v, w2_v):
    kc = compress(nsa_rows[:, :HEAD_DIM], pe_k, w1_k, w2_k)
    vc = compress(nsa_rows[:, HEAD_DIM:2 * HEAD_DIM], pe_v, w1_v, w2_v)
    n_cmp = kc.shape[0]
    c_start = jnp.arange(n_cmp, dtype=jnp.int32) * CMP_STRIDE
    c_end = c_start + CMP_BLOCK - 1
    kb = to_blocks(nsa_rows[:, 2 * HEAD_DIM:3 * HEAD_DIM])
    vb = to_blocks(nsa_rows[:, 3 * HEAD_DIM:])
    b_start = jnp.arange(kb.shape[0], dtype=jnp.int32) * SEL_BLOCK
    ov = ((c_start[:, None] < b_start[None, :] + SEL_BLOCK)
          & (c_start[:, None] + CMP_BLOCK > b_start[None, :])).astype(jnp.float32)
    return kc, vc, c_end, ov, kb, vb


def nsa_core(q, kc, vc, c_end, ov, kb, vb, kw, vw, w_pos, q_pos, rel_bias):
    scale = HEAD_DIM ** -0.5
    tq, h = q.shape[0], q.shape[1]
    dist_c = q_pos[:, None] - c_end[None, :]
    s_c = jnp.einsum('qhd,cd->qhc', q, kc).astype(jnp.float32) * scale + rel_bias_at(rel_bias, dist_c)
    p_c = masked_softmax(s_c, (dist_c >= 0)[:, None, :])
    o_cmp = jnp.einsum('qhc,cd->qhd', p_c.astype(vc.dtype), vc)
    n_sel = kb.shape[0]
    imp = jnp.einsum('qhc,cb->qb', p_c, ov)
    blk = jnp.arange(n_sel, dtype=jnp.int32)[None, :]
    cur = (q_pos // SEL_BLOCK)[:, None]
    visible = blk <= cur
    forced = (blk == 0) | (blk == cur) | (blk == cur - 1)
    score = jnp.where(visible, jnp.where(forced, FORCE_SCORE, imp), -1.0)
    top_s, idx = lax.top_k(score, min(SEL_TOP_N, n_sel))
    k_sel = kb[idx]
    v_sel = vb[idx]
    pos_sel = idx[..., None] * SEL_BLOCK + jnp.arange(SEL_BLOCK, dtype=jnp.int32)
    dist_s = q_pos[:, None, None] - pos_sel
    mask_s = (top_s >= 0.0)[..., None] & (dist_s >= 0)
    s_s = jnp.einsum('qhd,qksd->qhks', q, k_sel).astype(jnp.float32) * scale + rel_bias_at(rel_bias, dist_s)
    p_s = masked_softmax(s_s.reshape(tq, h, -1), mask_s.reshape(tq, 1, -1))
    o_slc = jnp.einsum('qhn,qnd->qhd', p_s.astype(v_sel.dtype), v_sel.reshape(tq, -1, v_sel.shape[-1]))
    dist_w = q_pos[:, None] - w_pos[None, :]
    mask_w = (dist_w >= 0) & (dist_w < WINDOW) & (w_pos[None, :] >= 0)
    s_w = jnp.einsum('qhd,wd->qhw', q, kw).astype(jnp.float32) * scale + rel_bias_at(rel_bias, dist_w)
    p_w = masked_softmax(s_w, mask_w[:, None, :])
    o_win = jnp.einsum('qhw,wd->qhd', p_w.astype(vw.dtype), vw)
    return jnp.stack([o_cmp, o_slc, o_win], axis=2)


def project(x, pos, g_attn, w_in, g_cq, g_ckv, w_uq, w_uk):
    b, t, _ = x.shape
    u = rmsnorm(x, g_attn) @ w_in
    cuts = [int(c) for c in np.cumsum(IN_SPLITS[:-1])]
    c_q, c_kv, k_pe, sb_q, sb_rows, nsa_q, nsa_rows, win_rows, gate_logits = jnp.split(u, cuts, axis=-1)
    q = jnp.einsum('btr,rhe->bthe', rmsnorm(c_q, g_cq), w_uq)
    q_pe = rope(q[..., MLA_NOPE:], pos)
    q_lat = jnp.einsum('bthn,chn->bthc', q[..., :MLA_NOPE], w_uk)
    mla_rows = jnp.concatenate([rmsnorm(c_kv, g_ckv), rope(k_pe, pos)], axis=-1)
    sb_q = sb_q.reshape(b, t, H_SB, HEAD_DIM)
    nsa_q = nsa_q.reshape(b, t, H_NSA, HEAD_DIM)
    gates = jax.nn.sigmoid(gate_logits.astype(jnp.float32)).reshape(b, t, H_NSA, 3).astype(x.dtype)
    return q_lat, q_pe, mla_rows, sb_q, sb_rows, nsa_q, nsa_rows, win_rows, gates


def finish(x, o_a_lat, o_b, o_c, gates, p, w_uv, g_grp_mla, g_grp_sb, g_grp_nsa, w_out,
           g_ffn, w_gate, w_up, w_down, g_ple, w_ple_gate, w_ple):
    b, t, _ = x.shape
    o_a = jnp.einsum('bthc,chv->bthv', o_a_lat, w_uv).reshape(b, t, -1)
    o_c = jnp.einsum('bthg,bthgd->bthd', gates, o_c).reshape(b, t, -1)
    mix = jnp.concatenate([rmsnorm(o_a, g_grp_mla), rmsnorm(o_b.reshape(b, t, -1), g_grp_sb),
                           rmsnorm(o_c, g_grp_nsa)], axis=-1)
    x = x + mix @ w_out
    h = rmsnorm(x, g_ffn)
    x = x + (jax.nn.silu(h @ w_gate) * (h @ w_up)) @ w_down
    gate = jax.nn.sigmoid(rmsnorm(x, g_ple) @ w_ple_gate)
    return x + gate * (p @ w_ple)


def prompt_mix_seq(q_lat, q_pe, mla_rows, sb_q, sb_rows, nsa_q, nsa_rows, win_rows,
                   pe_k, w1_k, w2_k, pe_v, w1_v, w2_v, rel_bias):
    t = q_lat.shape[0]
    pos = jnp.arange(t, dtype=jnp.int32)
    ckv, kpe = mla_rows[:, :MLA_KV_RANK], mla_rows[:, MLA_KV_RANK:]
    sb_k, sb_v = sb_rows[:, :HEAD_DIM], sb_rows[:, HEAD_DIM:]
    kc, vc, c_end, ov, kb, vb = nsa_prepare(nsa_rows, pe_k, w1_k, w2_k, pe_v, w1_v, w2_v)
    win_pad = jnp.pad(win_rows, ((WINDOW, 0), (0, 0)))

    def one_block(start):
        qp = start + jnp.arange(QBLK, dtype=jnp.int32)
        sl = lambda a: lax.dynamic_slice_in_dim(a, start, QBLK, axis=0)
        wk = lax.dynamic_slice_in_dim(win_pad, start, QBLK + WINDOW, axis=0)
        w_pos = start - WINDOW + jnp.arange(QBLK + WINDOW, dtype=jnp.int32)
        o_a = mla_core(sl(q_lat), sl(q_pe), ckv, kpe, qp, pos)
        o_b = sb_core(sl(sb_q), sb_k, sb_v, qp, pos)
        o_c = nsa_core(sl(nsa_q), kc, vc, c_end, ov, kb, vb, wk[:, :HEAD_DIM], wk[:, HEAD_DIM:],
                       w_pos, qp, rel_bias)
        return o_a, o_b, o_c

    starts = jnp.arange(t // QBLK, dtype=jnp.int32) * QBLK
    o_a, o_b, o_c = lax.map(one_block, starts)
    merge = lambda o: o.reshape((t,) + o.shape[2:])
    return merge(o_a), merge(o_b), merge(o_c)


def sample_mix(layer, cache_mla, cache_sb, cache_nsa, win_buf, page_table, q_lat, q_pe, mla_new,
               sb_q, sb_new, nsa_q, nsa_new, win_new, pe_k, w1_k, w2_k, pe_v, w1_v, w2_v, rel_bias):
    past = page_table.shape[1] * cache_mla.shape[2]
    n_new = q_lat.shape[1]
    k_pos = jnp.arange(past + n_new, dtype=jnp.int32)
    qp = past + jnp.arange(n_new, dtype=jnp.int32)
    buf_len = win_buf.shape[1]
    w_pos = past - buf_len + jnp.arange(buf_len + n_new, dtype=jnp.int32)

    def gather(cache, pt, new):
        rows = cache[layer, pt]
        return jnp.concatenate([rows.reshape(-1, rows.shape[-1]), new], axis=0)

    def one_seq(args):
        pt, ql, qpe, mla_n, sq, sb_n, nq, nsa_n, w_n, buf = args
        mla_rows = gather(cache_mla, pt, mla_n)
        sb_rows = gather(cache_sb, pt, sb_n)
        nsa_rows = gather(cache_nsa, pt, nsa_n)
        win_rows = jnp.concatenate([buf, w_n], axis=0)
        o_a = mla_core(ql, qpe, mla_rows[:, :MLA_KV_RANK], mla_rows[:, MLA_KV_RANK:], qp, k_pos)
        o_b = sb_core(sq, sb_rows[:, :HEAD_DIM], sb_rows[:, HEAD_DIM:], qp, k_pos)
        kc, vc, c_end, ov, kb, vb = nsa_prepare(nsa_rows, pe_k, w1_k, w2_k, pe_v, w1_v, w2_v)
        o_c = nsa_core(nq, kc, vc, c_end, ov, kb, vb, win_rows[:, :HEAD_DIM], win_rows[:, HEAD_DIM:],
                       w_pos, qp, rel_bias)
        return o_a, o_b, o_c, win_rows[n_new:]

    return lax.map(one_seq, (page_table, q_lat, q_pe, mla_new, sb_q, sb_new, nsa_q, nsa_new,
                             win_new, win_buf))


def setup_inputs(seed: int = 0) -> dict:
    key = jax.random.key(seed)
    ks = iter(jax.random.split(key, 48))
    nrm = lambda shape, scale=1.0: jax.random.normal(next(ks), shape, jnp.float32) * scale
    gain = lambda shape: 1.0 + nrm(shape, 0.05)
    n_pages = PAST_LEN // PAGE_SIZE
    n_used = DEC_BATCH * n_pages
    n_pool = (n_used * POOL_NUM) // POOL_DEN
    win_buf = min(WINDOW, PAST_LEN)
    page_table = jax.random.permutation(next(ks), n_pool)[:n_used].reshape(DEC_BATCH, n_pages).astype(jnp.int32)
    d = D_MODEL
    return {
        'x_prompt': nrm((BATCH, SEQ, d)),
        'x_sample': nrm((DEC_BATCH, DEC_SEQ, d)),
        'cache_mla': nrm((DEPTH, n_pool, PAGE_SIZE, MLA_KV_RANK + MLA_ROPE)),
        'cache_sb': nrm((DEPTH, n_pool, PAGE_SIZE, 2 * HEAD_DIM)),
        'cache_nsa': nrm((DEPTH, n_pool, PAGE_SIZE, 4 * HEAD_DIM)),
        'state_win': nrm((DEPTH, DEC_BATCH, win_buf, 2 * HEAD_DIM)),
        'page_table': page_table,
        'p_prompt': nrm((DEPTH, BATCH, SEQ, PLE_DIM)),
        'p_sample': nrm((DEPTH, DEC_BATCH, DEC_SEQ, PLE_DIM)),
        'g_attn': gain((DEPTH, d)),
        'w_in': nrm((DEPTH, d, D_IN), d ** -0.5),
        'g_cq': gain((DEPTH, MLA_Q_RANK)),
        'g_ckv': gain((DEPTH, MLA_KV_RANK)),
        'w_uq': nrm((DEPTH, MLA_Q_RANK, H_MLA, MLA_NOPE + MLA_ROPE), MLA_Q_RANK ** -0.5),
        'w_uk': nrm((DEPTH, MLA_KV_RANK, H_MLA, MLA_NOPE), MLA_KV_RANK ** -0.5),
        'w_uv': nrm((DEPTH, MLA_KV_RANK, H_MLA, MLA_V), MLA_KV_RANK ** -0.5),
        'cmp_pe_k': nrm((DEPTH, CMP_BLOCK, HEAD_DIM), 0.02),
        'cmp_w1_k': nrm((DEPTH, CMP_BLOCK * HEAD_DIM, CMP_HIDDEN), (CMP_BLOCK * HEAD_DIM) ** -0.5),
        'cmp_w2_k': nrm((DEPTH, CMP_HIDDEN, HEAD_DIM), CMP_HIDDEN ** -0.5),
        'cmp_pe_v': nrm((DEPTH, CMP_BLOCK, HEAD_DIM), 0.02),
        'cmp_w1_v': nrm((DEPTH, CMP_BLOCK * HEAD_DIM, CMP_HIDDEN), (CMP_BLOCK * HEAD_DIM) ** -0.5),
        'cmp_w2_v': nrm((DEPTH, CMP_HIDDEN, HEAD_DIM), CMP_HIDDEN ** -0.5),
        'rel_bias': nrm((N_BUCKETS, H_NSA), 0.1),
        'g_grp_mla': gain((DEPTH, H_MLA * MLA_V)),
        'g_grp_sb': gain((DEPTH, H_SB * HEAD_DIM)),
        'g_grp_nsa': gain((DEPTH, H_NSA * HEAD_DIM)),
        'w_out': nrm((DEPTH, D_MIX, d), D_MIX ** -0.5),
        'g_ffn': gain((DEPTH, d)),
        'w_gate': nrm((DEPTH, d, D_FF), d ** -0.5),
        'w_up': nrm((DEPTH, d, D_FF), d ** -0.5),
        'w_down': nrm((DEPTH, D_FF, d), D_FF ** -0.5),
        'g_ple': gain((DEPTH, d)),
        'w_ple_gate': nrm((DEPTH, d, d), d ** -0.5),
        'w_ple': nrm((DEPTH, PLE_DIM, d), PLE_DIM ** -0.5),
        'g_final': gain((d,)),
    }


def reference(x_prompt, x_sample, cache_mla, cache_sb, cache_nsa, state_win, page_table,
              p_prompt, p_sample, g_attn, w_in, g_cq, g_ckv, w_uq, w_uk, w_uv,
              cmp_pe_k, cmp_w1_k, cmp_w2_k, cmp_pe_v, cmp_w1_v, cmp_w2_v, rel_bias,
              g_grp_mla, g_grp_sb, g_grp_nsa, w_out, g_ffn, w_gate, w_up, w_down,
              g_ple, w_ple_gate, w_ple, g_final):
    seq = x_prompt.shape[1]
    past = page_table.shape[1] * cache_mla.shape[2]
    pos_p = jnp.arange(seq, dtype=jnp.int32)
    pos_s = past + jnp.arange(x_sample.shape[1], dtype=jnp.int32)
    win_keep = min(WINDOW, seq)
    prompt_mix = jax.vmap(prompt_mix_seq, in_axes=(0,) * 8 + (None,) * 7)
    xp, xs = x_prompt, x_sample
    mla_p, mla_s, sb_p, sb_s, nsa_p, nsa_s, win_p, win_s = [], [], [], [], [], [], [], []
    for i in range(DEPTH):
        proj_w = (g_attn[i], w_in[i], g_cq[i], g_ckv[i], w_uq[i], w_uk[i])
        cmp_w = (cmp_pe_k[i], cmp_w1_k[i], cmp_w2_k[i], cmp_pe_v[i], cmp_w1_v[i], cmp_w2_v[i], rel_bias)
        out_w = (w_uv[i], g_grp_mla[i], g_grp_sb[i], g_grp_nsa[i], w_out[i], g_ffn[i], w_gate[i],
                 w_up[i], w_down[i], g_ple[i], w_ple_gate[i], w_ple[i])
        ql, qpe, mrow, sbq, sbrow, nq, nrow, wrow, gts = project(xp, pos_p, *proj_w)
        o_a, o_b, o_c = prompt_mix(ql, qpe, mrow, sbq, sbrow, nq, nrow, wrow, *cmp_w)
        xp = finish(xp, o_a, o_b, o_c, gts, p_prompt[i], *out_w)
        mla_p.append(mrow)
        sb_p.append(sbrow)
        nsa_p.append(nrow)
        win_p.append(wrow[:, seq - win_keep:])
        ql, qpe, mrow, sbq, sbrow, nq, nrow, wrow, gts = project(xs, pos_s, *proj_w)
        o_a, o_b, o_c, new_buf = sample_mix(i, cache_mla, cache_sb, cache_nsa, state_win[i], page_table,
                                            ql, qpe, mrow, sbq, sbrow, nq, nrow, wrow, *cmp_w)
        xs = finish(xs, o_a, o_b, o_c, gts, p_sample[i], *out_w)
        mla_s.append(mrow)
        sb_s.append(sbrow)
        nsa_s.append(nrow)
        win_s.append(new_buf)
    y_prompt = rmsnorm(xp, g_final)
    y_sample = rmsnorm(xs, g_final)
    return (y_prompt, y_sample, jnp.stack(mla_p), jnp.stack(mla_s), jnp.stack(sb_p), jnp.stack(sb_s),
            jnp.stack(nsa_p), jnp.stack(nsa_s), jnp.stack(win_p), jnp.stack(win_s))
```

```python
import functools
import math

import numpy as np
import jax
import jax.numpy as jnp
from jax import lax
from jax.experimental import pallas as pl
from jax.experimental.pallas import tpu as pltpu

F32 = jnp.float32
BF16 = jnp.bfloat16

HEAD_DIM = 64
H_MLA = 6
H_SB = 4
H_NSA = 6
MLA_Q_RANK = 256
MLA_KV_RANK = 256
MLA_NOPE = 64
MLA_ROPE = 32
MLA_V = 64
MLA_SCALE = (MLA_NOPE + MLA_ROPE) ** -0.5
ATT_SCALE = HEAD_DIM ** -0.5
ROPE_BASE = 10000.0
CMP_BLOCK = 32
CMP_STRIDE = 16
SEL_BLOCK = 64
SEL_TOP_N = 16
FORCE_SCORE = 1e4
WINDOW = 512
N_BUCKETS = 32
MAX_DISTANCE = 128
EPS = 1e-6
NEG_BIG = -1e30
IN_SPLITS = [MLA_Q_RANK, MLA_KV_RANK, MLA_ROPE, H_SB * HEAD_DIM, 2 * HEAD_DIM,
             H_NSA * HEAD_DIM, 4 * HEAD_DIM, 2 * HEAD_DIM, 3 * H_NSA]

LANE = 128
SLOT = 128
ROPE_HALF = MLA_ROPE // 2
VMEM_LIMIT = 56 * 1024 * 1024
PAGES_PER_STEP = 16
SB_CHUNK = 256
ROWS_Q = 32
NEW_PAD = 16


def _cparams(sem):
    return pltpu.CompilerParams(dimension_semantics=sem, vmem_limit_bytes=VMEM_LIMIT)


def _rms(x, g):
    return x * lax.rsqrt(jnp.mean(x * x, axis=-1, keepdims=True) + EPS) * g


def _dot(a, b):
    return jnp.dot(a, b, preferred_element_type=F32)


def _dot_nt(a, b):
    return lax.dot_general(a, b, (((1,), (1,)), ((), ())), preferred_element_type=F32)


def _dot_hilo(x, w):
    hi = x.astype(BF16)
    lo = (x - hi.astype(F32)).astype(BF16)
    return _dot(hi, w) + _dot(lo, w)


def _gelu_tanh(x):
    return 0.5 * x * (1.0 + jnp.tanh(math.sqrt(2.0 / math.pi) * (x + 0.044715 * (x * x * x))))


def _full(shape):
    n = len(shape)
    return pl.BlockSpec(shape, lambda *_: (0,) * n)


def _t5_bucket_np(dist):
    n = np.maximum(dist, 0)
    max_exact = N_BUCKETS // 2
    nf = np.maximum(n, 1).astype(np.float32)
    large = max_exact + (np.log(nf / np.float32(max_exact)) / np.float32(math.log(MAX_DISTANCE / max_exact))
                         * np.float32(N_BUCKETS - max_exact)).astype(np.int32)
    large = np.minimum(large, N_BUCKETS - 1)
    return np.where(n < max_exact, n, large).astype(np.int32)


def _bias_rows(rel_bias, dist, rows_per_q):
    nq, k = dist.shape
    b = rel_bias[_t5_bucket_np(dist)]
    b = jnp.transpose(b, (0, 2, 1))
    b = jnp.pad(b, ((0, 0), (0, rows_per_q - H_NSA), (0, 0)))
    return b.reshape(nq * rows_per_q, k).astype(F32)


def _rope_tables(pos):
    freqs = ROPE_BASE ** (-jnp.arange(ROPE_HALF, dtype=F32) / ROPE_HALF)
    ang = pos.astype(F32)[:, None] * freqs[None, :]
    cos, sin = jnp.cos(ang), jnp.sin(ang)
    n = pos.shape[0]
    one = jnp.ones((n, MLA_NOPE), F32)
    zq = jnp.zeros((n, SLOT - MLA_NOPE - MLA_ROPE), F32)
    cq = jnp.concatenate([one, cos, cos, zq], axis=1)
    sq = jnp.concatenate([0 * one, sin, sin, zq], axis=1)
    zk = jnp.zeros((n, LANE - MLA_ROPE), F32)
    ck = jnp.concatenate([cos, cos, zk], axis=1)
    sk = jnp.concatenate([sin, sin, zk], axis=1)
    return cq, sq, ck, sk


U_CQ, U_CKV, U_SBQ, U_SBR, U_NQ, U_NR, U_WIN, U_GATE, U_KPE, U_KPES, U_END = (
    0, 256, 512, 768, 896, 1280, 1536, 1664, 1792, 1920, 2048)


def _proj_weights(w_in, w_uq, w_uk, w_uv):
    cuts = [int(c) for c in np.cumsum(IN_SPLITS[:-1])]
    cq, ckv, kpe, sbq, sbr, nq, nr, wr, gt = jnp.split(w_in, cuts, axis=1)
    pad = lambda w, n: jnp.pad(w, ((0, 0), (0, n - w.shape[1])))
    kpe_sw = jnp.concatenate([-kpe[:, ROPE_HALF:], kpe[:, :ROPE_HALF]], axis=1)
    w_all = jnp.concatenate([cq, ckv, sbq, sbr, nq, nr, wr, pad(gt, LANE), pad(kpe, LANE), pad(kpe_sw, LANE)],
                            axis=1).astype(BF16)
    r = w_uq.shape[0]
    zpad = jnp.zeros((r, H_MLA, SLOT - MLA_NOPE - MLA_ROPE), F32)
    wq = jnp.concatenate([w_uq, zpad], axis=2).reshape(r, H_MLA * SLOT).astype(BF16)
    pe1 = w_uq[:, :, MLA_NOPE:MLA_NOPE + ROPE_HALF]
    pe2 = w_uq[:, :, MLA_NOPE + ROPE_HALF:]
    wqs = jnp.concatenate([jnp.zeros((r, H_MLA, MLA_NOPE), F32), -pe2, pe1, zpad], axis=2)
    wqs = wqs.reshape(r, H_MLA * SLOT).astype(BF16)
    c = w_uk.shape[0]
    wk = jnp.concatenate([w_uk, jnp.zeros((c, H_MLA, SLOT - MLA_NOPE), F32)], axis=2)
    wk = wk.reshape(c, H_MLA * SLOT).astype(BF16)
    wv = w_uv.reshape(c, H_MLA * MLA_V).astype(BF16)
    wukt = jnp.transpose(w_uk, (1, 2, 0))
    wukt = jnp.pad(wukt, ((0, 0), (0, SLOT - MLA_NOPE), (0, 0)))
    eye = jnp.eye(H_MLA, dtype=F32)
    wukbd = (eye[:, None, :, None] * wukt[:, :, None, :]).reshape(H_MLA * SLOT, H_MLA * c).astype(BF16)
    return w_all, wq, wqs, wk, wv, wukbd


def _proj_kernel(*refs, prompt):
    (x_ref, gat_ref, wall_ref, gcq_ref, gckv_ref, wq_ref, wqs_ref, cq_ref, sq_ref, ck_ref, sk_ref) = refs[:11]
    if prompt:
        wk_ref, wv_ref = refs[11:13]
        mla_o, sb_o, nsa_o, win_o, gate_o, sbq_o, nq_o, qa_o, ka_o, va_o = refs[13:]
    else:
        (wukbd_ref,) = refs[11:12]
        mla_o, sb_o, nsa_o, win_o, gate_o, sbq_o, nq_o, qa_o, qlat_o = refs[12:]
    xn = _rms(x_ref[...], gat_ref[...]).astype(BF16)
    u = _dot(xn, wall_ref[...])
    sbq_o[...] = u[:, U_SBQ:U_SBR].astype(BF16)
    sb_o[...] = u[:, U_SBR:U_NQ]
    nq_o[...] = u[:, U_NQ:U_NR].astype(BF16)
    nsa_o[...] = u[:, U_NR:U_WIN]
    win_o[...] = u[:, U_WIN:U_GATE]
    gate_o[...] = jax.nn.sigmoid(u[:, U_GATE:U_KPE])
    kr = u[:, U_KPE:U_KPES] * ck_ref[...] + u[:, U_KPES:U_END] * sk_ref[...]
    ckvn = _rms(u[:, U_CKV:U_SBQ], gckv_ref[...])
    mla_o[:, 0:MLA_KV_RANK] = ckvn
    mla_o[:, MLA_KV_RANK:MLA_KV_RANK + MLA_ROPE] = kr[:, 0:MLA_ROPE]
    cqn = _rms(u[:, U_CQ:U_CKV], gcq_ref[...]).astype(BF16)
    qf = _dot(cqn, wq_ref[...])
    qs = _dot(cqn, wqs_ref[...])
    cq_t, sq_t = cq_ref[...], sq_ref[...]
    qrot = []
    for h in range(H_MLA):
        sl = slice(h * SLOT, (h + 1) * SLOT)
        qh = (qf[:, sl] * cq_t + qs[:, sl] * sq_t).astype(BF16)
        qa_o[:, sl] = qh
        qrot.append(qh)
    if prompt:
        ckb = ckvn.astype(BF16)
        kn = _dot(ckb, wk_ref[...])
        krr = pltpu.roll(kr, MLA_NOPE, axis=1)
        for h in range(H_MLA):
            sl = slice(h * SLOT, (h + 1) * SLOT)
            ka_o[:, sl] = (kn[:, sl] + krr).astype(BF16)
        va_o[...] = _dot(ckb, wv_ref[...]).astype(BF16)
    else:
        qlat_o[...] = _dot(jnp.concatenate(qrot, axis=1), wukbd_ref[...]).astype(BF16)


def _proj(x, tables, pw, g_attn, g_cq, g_ckv, *, prompt, tm, table_period):
    n, d = x.shape
    w_all, wq, wqs, wk, wv, wukbd = pw
    cq_t, sq_t, ck_t, sk_t = tables
    nt = table_period // tm
    row = lambda w: pl.BlockSpec((tm, w), lambda i: (i, 0))
    tab = pl.BlockSpec((tm, LANE), lambda i: (i % nt, 0))
    ins = [x, g_attn.reshape(1, d), w_all, g_cq.reshape(1, -1), g_ckv.reshape(1, -1), wq, wqs, cq_t, sq_t, ck_t, sk_t]
    in_specs = [row(d), _full((1, d)), _full(w_all.shape), _full((1, MLA_Q_RANK)), _full((1, MLA_KV_RANK)),
                _full(wq.shape), _full(wqs.shape), tab, tab, tab, tab]
    outs = [(MLA_KV_RANK + MLA_ROPE, F32), (2 * HEAD_DIM, F32), (4 * HEAD_DIM, F32), (2 * HEAD_DIM, F32),
            (LANE, F32), (H_SB * HEAD_DIM, BF16), (H_NSA * HEAD_DIM, BF16), (H_MLA * SLOT, BF16)]
    if prompt:
        ins += [wk, wv]
        in_specs += [_full(wk.shape), _full(wv.shape)]
        outs += [(H_MLA * SLOT, BF16), (H_MLA * MLA_V, BF16)]
    else:
        ins += [wukbd]
        in_specs += [_full(wukbd.shape)]
        outs += [(H_MLA * MLA_KV_RANK, BF16)]
    return pl.pallas_call(
        functools.partial(_proj_kernel, prompt=prompt),
        grid=(n // tm,),
        in_specs=in_specs,
        out_specs=[row(w) for w, _ in outs],
        out_shape=[jax.ShapeDtypeStruct((n, w), dt) for w, dt in outs],
        compiler_params=_cparams(("parallel",)),
        name="proj_prompt" if prompt else "proj_sample",
    )(*ins)


def _finish_kernel(x_ref, oa_ref, ob_ref, oc_ref, p_ref, ga_ref, gb_ref, gc_ref, woa_ref, wob_ref, woc_ref,
                   gffn_ref, wg_ref, wu_ref, wd_ref, gple_ref, wpg_ref, wple_ref, gfin_ref, out_ref,
                   *, final, n_chunk):
    mix = (_dot(_rms(oa_ref[...], ga_ref[...]).astype(BF16), woa_ref[...])
           + _dot(_rms(ob_ref[...], gb_ref[...]).astype(BF16), wob_ref[...])
           + _dot(_rms(oc_ref[...], gc_ref[...]).astype(BF16), woc_ref[...]))
    x1 = x_ref[...] + mix
    h = _rms(x1, gffn_ref[...]).astype(BF16)
    fc = wg_ref.shape[1] // n_chunk
    ff = jnp.zeros_like(x1)
    for c in range(n_chunk):
        g = _dot(h, wg_ref[:, c * fc:(c + 1) * fc])
        u = _dot(h, wu_ref[:, c * fc:(c + 1) * fc])
        ff = ff + _dot((g * jax.nn.sigmoid(g) * u).astype(BF16), wd_ref[c * fc:(c + 1) * fc, :])
    x2 = x1 + ff
    pg = jax.nn.sigmoid(_dot(_rms(x2, gple_ref[...]).astype(BF16), wpg_ref[...]))
    x3 = x2 + pg * _dot(p_ref[...].astype(BF16), wple_ref[...])
    out_ref[...] = _rms(x3, gfin_ref[...]) if final else x3


def _finish(x, oa, ob, oc, p, fw, *, final, tm):
    n, d = x.shape
    (ga, gb, gc, woa, wob, woc, gffn, wg, wu, wd, gple, wpg, wple, gfin) = fw
    row = lambda w: pl.BlockSpec((tm, w), lambda i: (i, 0))
    const = lambda a: pl.BlockSpec(a.shape, lambda i: (0,) * a.ndim, pipeline_mode=pl.Buffered(1))
    ws = [ga, gb, gc, woa, wob, woc, gffn, wg, wu, wd, gple, wpg, wple, gfin]
    d_ff = wg.shape[1]
    n_chunk = 2 if d_ff % (2 * LANE) == 0 else 1
    return pl.pallas_call(
        functools.partial(_finish_kernel, final=final, n_chunk=n_chunk),
        grid=(n // tm,),
        in_specs=[row(d), row(oa.shape[1]), row(ob.shape[1]), row(oc.shape[1]), row(p.shape[1])] + [const(w) for w in ws],
        out_specs=row(d),
        out_shape=jax.ShapeDtypeStruct((n, d), F32),
        compiler_params=_cparams(("parallel",)),
        name="finish",
    )(x, oa, ob, oc, p, *ws)


def _finish_weights(g_grp_mla, g_grp_sb, g_grp_nsa, w_out, g_ffn, w_gate, w_up, w_down, g_ple, w_ple_gate, w_ple,
                    g_final):
    na, nb = H_MLA * MLA_V, H_SB * HEAD_DIM
    r = lambda g: g.reshape(1, -1)
    return (r(g_grp_mla), r(g_grp_sb), r(g_grp_nsa), w_out[:na].astype(BF16), w_out[na:na + nb].astype(BF16),
            w_out[na + nb:].astype(BF16), r(g_ffn), w_gate.astype(BF16), w_up.astype(BF16), w_down.astype(BF16),
            r(g_ple), w_ple_gate.astype(BF16), w_ple.astype(BF16), r(g_final))


def _softmax_step(s, mask, m, l):
    s = jnp.where(mask, s, NEG_BIG)
    m_new = jnp.maximum(m, jnp.max(s, axis=1, keepdims=True))
    p = jnp.where(mask, jnp.exp(s - m_new), 0.0)
    alpha = jnp.exp(m - m_new)
    return p, alpha, m_new, alpha * l + jnp.sum(p, axis=1, keepdims=True)


def _mla_prompt_kernel(q_ref, k_ref, v_ref, o_ref, *, tq):
    qi = pl.program_id(1)
    rows = qi * tq + lax.broadcasted_iota(jnp.int32, (tq, 1), 0)
    cols0 = lax.broadcasted_iota(jnp.int32, (1, tq), 1)
    for h in range(H_MLA):
        q = q_ref[:, h * SLOT:(h + 1) * SLOT]

        def body(kt, carry, q=q, h=h):
            m, l, acc = carry
            k0 = pl.multiple_of(kt * tq, tq)
            k = k_ref[pl.ds(k0, tq), h * SLOT:(h + 1) * SLOT]
            v = v_ref[pl.ds(k0, tq), h * MLA_V:(h + 1) * MLA_V]
            s = _dot_nt(q, k) * MLA_SCALE
            p, alpha, m, l = _softmax_step(s, (k0 + cols0) <= rows, m, l)
            return m, l, alpha * acc + _dot(p.astype(BF16), v)

        init = (jnp.full((tq, 1), NEG_BIG, F32), jnp.zeros((tq, 1), F32), jnp.zeros((tq, MLA_V), F32))
        m, l, acc = lax.fori_loop(0, qi + 1, body, init)
        o_ref[:, h * MLA_V:(h + 1) * MLA_V] = acc / jnp.maximum(l, 1e-30)


def _mla_prompt(qa, ka, va, *, b, t, tq):
    nq = t // tq
    return pl.pallas_call(
        functools.partial(_mla_prompt_kernel, tq=tq),
        grid=(b, nq),
        in_specs=[pl.BlockSpec((tq, H_MLA * SLOT), lambda i, j: (i * nq + j, 0)),
                  pl.BlockSpec((t, H_MLA * SLOT), lambda i, j: (i, 0)),
                  pl.BlockSpec((t, H_MLA * MLA_V), lambda i, j: (i, 0))],
        out_specs=pl.BlockSpec((tq, H_MLA * MLA_V), lambda i, j: (i * nq + j, 0)),
        out_shape=jax.ShapeDtypeStruct((b * t, H_MLA * MLA_V), F32),
        compiler_params=_cparams(("parallel", "arbitrary")),
        name="mla_prompt",
    )(qa, ka, va)


def _sb_tile(z, mask, r_sum, v, u_tri):
    sp = jnp.log(1.0 + jnp.exp(-jnp.abs(z)))
    log_keep = -jnp.maximum(z, 0.0) - sp
    log_beta = jnp.minimum(z, 0.0) - sp
    if mask is not None:
        log_keep = jnp.where(mask, log_keep, 0.0)
    excl = _dot_hilo(log_keep, u_tri)
    a = jnp.exp(log_beta + excl + r_sum)
    if mask is not None:
        a = jnp.where(mask, a, 0.0)
    return _dot(a.astype(BF16), v), r_sum + jnp.sum(log_keep, axis=1, keepdims=True)


def _sb_prompt_kernel(q_ref, kv_ref, u_ref, o_ref, *, tq):
    qi = pl.program_id(1)
    rows = qi * tq + lax.broadcasted_iota(jnp.int32, (tq, 1), 0)
    cols0 = lax.broadcasted_iota(jnp.int32, (1, tq), 1)
    u_tri = u_ref[...]
    for h in range(H_SB):
        q = q_ref[:, h * HEAD_DIM:(h + 1) * HEAD_DIM]

        def body(i, carry, q=q):
            r_sum, acc = carry
            k0 = pl.multiple_of((qi - i) * tq, tq)
            kv = kv_ref[pl.ds(k0, tq), :].astype(BF16)
            z = _dot_nt(q, kv[:, :HEAD_DIM]) * ATT_SCALE
            o, r_sum = _sb_tile(z, (k0 + cols0) < rows, r_sum, kv[:, HEAD_DIM:], u_tri)
            return r_sum, acc + o

        init = (jnp.zeros((tq, 1), F32), jnp.zeros((tq, HEAD_DIM), F32))
        _, acc = lax.fori_loop(0, qi + 1, body, init)
        o_ref[:, h * HEAD_DIM:(h + 1) * HEAD_DIM] = acc


def _sb_prompt(sbq, sb_rows, u_tri, *, b, t, tq):
    nq = t // tq
    return pl.pallas_call(
        functools.partial(_sb_prompt_kernel, tq=tq),
        grid=(b, nq),
        in_specs=[pl.BlockSpec((tq, H_SB * HEAD_DIM), lambda i, j: (i * nq + j, 0)),
                  pl.BlockSpec((t, 2 * HEAD_DIM), lambda i, j: (i, 0)),
                  _full(u_tri.shape)],
        out_specs=pl.BlockSpec((tq, H_SB * HEAD_DIM), lambda i, j: (i * nq + j, 0)),
        out_shape=jax.ShapeDtypeStruct((b * t, H_SB * HEAD_DIM), F32),
        compiler_params=_cparams(("parallel", "arbitrary")),
        name="sb_prompt",
    )(sbq, sb_rows, u_tri)


def _compress_weights(pe_k, w1_k, pe_v, w1_v):
    d = HEAD_DIM
    w1k = w1_k.reshape(CMP_BLOCK, d, -1)
    w1v = w1_v.reshape(CMP_BLOCK, d, -1)
    z = jnp.zeros((d, d), F32)
    mats, pes = [], []
    for i in range(CMP_STRIDE // 2):
        blocks = []
        for r in (2 * i, 2 * i + 1):
            blocks.append(jnp.concatenate([w1k[r], w1k[CMP_STRIDE + r], z, z], axis=1))
            blocks.append(jnp.concatenate([z, z, w1v[r], w1v[CMP_STRIDE + r]], axis=1))
        mats.append(jnp.concatenate(blocks, axis=0))
        row_a = jnp.concatenate([pe_k[2 * i], pe_v[2 * i], pe_k[2 * i + 1], pe_v[2 * i + 1]])
        row_b = jnp.concatenate([pe_k[CMP_STRIDE + 2 * i], pe_v[CMP_STRIDE + 2 * i],
                                 pe_k[CMP_STRIDE + 2 * i + 1], pe_v[CMP_STRIDE + 2 * i + 1]])
        pes.append(jnp.concatenate([row_a[None], row_b[None], jnp.zeros((6, 4 * d), F32)], axis=0))
    return jnp.stack(mats).astype(BF16), jnp.stack(pes).astype(BF16)


def _compress_bias(pe_ref, wc_ref):
    acc = jnp.zeros((8, 4 * HEAD_DIM), F32)
    for i in range(CMP_STRIDE // 2):
        acc = acc + _dot(pe_ref[i], wc_ref[i])
    lane = lax.broadcasted_iota(jnp.int32, (1, 4 * HEAD_DIM), 1)
    is_a = (lane & HEAD_DIM) == 0
    return jnp.where(is_a, acc[0:1, :], acc[1:2, :])


def _compress_finish(ab, w2k_ref, w2v_ref):
    n_sub = ab.shape[0]
    nxt = pltpu.roll(ab, n_sub - 1, axis=0)
    d = HEAD_DIM
    hid_k = _gelu_tanh(ab[:, 0:d] + nxt[:, d:2 * d])
    hid_v = _gelu_tanh(ab[:, 2 * d:3 * d] + nxt[:, 3 * d:4 * d])
    kc = _dot(hid_k.astype(BF16), w2k_ref[...]).astype(BF16)
    vc = _dot(hid_v.astype(BF16), w2v_ref[...]).astype(BF16)
    return kc, vc


def _select_blocks(imp, cur, n_blk):
    blk = lax.broadcasted_iota(jnp.int32, imp.shape, 1)
    visible = blk <= cur
    forced = (blk == 0) | (blk == cur) | (blk == cur - 1)
    score = jnp.where(visible, jnp.where(forced, FORCE_SCORE, imp), -1.0)
    rank = jnp.zeros(imp.shape, jnp.int32)
    for b2 in range(n_blk):
        sb2 = score[:, b2:b2 + 1]
        ahead = (sb2 > score) | ((sb2 == score) & (b2 < blk))
        rank = rank + ahead.astype(jnp.int32)
    return ((rank < SEL_TOP_N) & (score >= 0.0)).astype(F32)


def _nsa_prompt_kernel(q_ref, cmp_ref, slc_ref, win_ref, gate_ref, wc_ref, pe_ref, w2k_ref, w2v_ref, ov_ref, esel_ref,
                       bc_ref, bd_ref, c31_ref, o_ref, kc_sc, vc_sc, *, tq, n_sub, n_blk):
    qi = pl.program_id(1)
    d = HEAD_DIM

    @pl.when(qi == 0)
    def _():
        ab = jnp.zeros((n_sub, 4 * d), F32)
        for i in range(CMP_STRIDE // 2):
            lhs = jnp.concatenate([cmp_ref[pl.ds(2 * i, n_sub, stride=CMP_STRIDE), :],
                                   cmp_ref[pl.ds(2 * i + 1, n_sub, stride=CMP_STRIDE), :]], axis=1)
            ab = ab + _dot(lhs.astype(BF16), wc_ref[i])
        kc, vc = _compress_finish(ab + _compress_bias(pe_ref, wc_ref), w2k_ref, w2v_ref)
        kc_sc[...] = kc
        vc_sc[...] = vc

    rows = qi * tq + lax.broadcasted_iota(jnp.int32, (tq, 1), 0)
    cols0 = lax.broadcasted_iota(jnp.int32, (1, tq), 1)
    gates = gate_ref[...]
    kc, vc = kc_sc[...], vc_sc[...]
    c_end = lax.broadcasted_iota(jnp.int32, (1, n_sub), 1) * CMP_STRIDE + (CMP_BLOCK - 1)
    mask_c = c_end <= rows

    o_cmp, psum = [], jnp.zeros((tq, n_sub), F32)
    for h in range(H_NSA):
        q = q_ref[:, h * d:(h + 1) * d]
        s = jnp.where(mask_c, _dot_nt(q, kc) * ATT_SCALE + bc_ref[h], NEG_BIG)
        e = jnp.where(mask_c, jnp.exp(s - jnp.max(s, axis=1, keepdims=True)), 0.0)
        p = e / jnp.maximum(jnp.sum(e, axis=1, keepdims=True), 1e-30)
        psum = psum + p
        o_cmp.append(_dot(p.astype(BF16), vc))
    imp = _dot_hilo(psum, ov_ref[...])
    sel = _select_blocks(imp, rows >> 6, n_blk).astype(BF16)

    def bias_tile(h, delta):
        near = jnp.where(delta == 0, bd_ref[0, h], bd_ref[1, h])
        return jnp.where(delta <= 1, near, c31_ref[h])

    for h in range(H_NSA):
        q = q_ref[:, h * d:(h + 1) * d]

        def slc_body(kt, carry, q=q, h=h):
            m, l, acc = carry
            k0 = pl.multiple_of(kt * tq, tq)
            kv = slc_ref[pl.ds(k0, tq), :].astype(BF16)
            picked = _dot(sel, esel_ref[kt]) > 0.5
            mask = picked & ((k0 + cols0) <= rows)
            s = _dot_nt(q, kv[:, :d]) * ATT_SCALE + bias_tile(h, qi - kt)
            p, alpha, m, l = _softmax_step(s, mask, m, l)
            return m, l, alpha * acc + _dot(p.astype(BF16), kv[:, d:])

        def win_body(kt, carry, q=q, h=h):
            m, l, acc = carry
            k0 = pl.multiple_of(kt * tq, tq)
            kv = win_ref[pl.ds(k0, tq), :].astype(BF16)
            dist = rows - (k0 + cols0)
            mask = (dist >= 0) & (dist < WINDOW)
            s = _dot_nt(q, kv[:, :d]) * ATT_SCALE + bias_tile(h, qi - kt)
            p, alpha, m, l = _softmax_step(s, mask, m, l)
            return m, l, alpha * acc + _dot(p.astype(BF16), kv[:, d:])

        init = (jnp.full((tq, 1), NEG_BIG, F32), jnp.zeros((tq, 1), F32), jnp.zeros((tq, d), F32))
        _, l_s, acc_s = lax.fori_loop(0, qi + 1, slc_body, init)
        _, l_w, acc_w = lax.fori_loop(jnp.maximum(qi - (WINDOW + tq - 2) // tq, 0), qi + 1, win_body, init)
        o_slc = acc_s / jnp.maximum(l_s, 1e-30)
        o_win = acc_w / jnp.maximum(l_w, 1e-30)
        o_ref[:, h * d:(h + 1) * d] = (gates[:, 3 * h:3 * h + 1] * o_cmp[h] + gates[:, 3 * h + 1:3 * h + 2] * o_slc
                                       + gates[:, 3 * h + 2:3 * h + 3] * o_win)


def _nsa_prompt(nq, nsa_rows, win_rows, gates, cw, tabs, *, b, t, tq):
    nqb = t // tq
    n_sub = t // CMP_STRIDE
    n_blk = -(-t // SEL_BLOCK)
    wc, pe, w2k, w2v = cw
    ov, esel, bias_c, bias_d, c31 = tabs
    d = HEAD_DIM
    return pl.pallas_call(
        functools.partial(_nsa_prompt_kernel, tq=tq, n_sub=n_sub, n_blk=n_blk),
        grid=(b, nqb),
        in_specs=[pl.BlockSpec((tq, H_NSA * d), lambda i, j: (i * nqb + j, 0)),
                  pl.BlockSpec((t, 2 * d), lambda i, j: (i, 0)),
                  pl.BlockSpec((t, 2 * d), lambda i, j: (i, 1)),
                  pl.BlockSpec((t, 2 * d), lambda i, j: (i, 0)),
                  pl.BlockSpec((tq, LANE), lambda i, j: (i * nqb + j, 0)),
                  _full(wc.shape), _full(pe.shape), _full(w2k.shape), _full(w2v.shape),
                  _full(ov.shape), _full(esel.shape),
                  pl.BlockSpec((H_NSA, tq, n_sub), lambda i, j: (0, j, 0)),
                  _full(bias_d.shape), _full(c31.shape)],
        out_specs=pl.BlockSpec((tq, H_NSA * d), lambda i, j: (i * nqb + j, 0)),
        out_shape=jax.ShapeDtypeStruct((b * t, H_NSA * d), F32),
        scratch_shapes=[pltpu.VMEM((n_sub, d), BF16), pltpu.VMEM((n_sub, d), BF16)],
        compiler_params=_cparams(("parallel", "arbitrary")),
        name="nsa_prompt",
    )(nq, nsa_rows, nsa_rows, win_rows, gates, wc, pe, w2k, w2v, ov, esel, bias_c, bias_d, c31)


def _overlap_matrix(n_cmp_pad, n_blk_pad):
    c_start = np.arange(n_cmp_pad)[:, None] * CMP_STRIDE
    b_start = np.arange(n_blk_pad)[None, :] * SEL_BLOCK
    return ((c_start < b_start + SEL_BLOCK) & (c_start + CMP_BLOCK > b_start)).astype(np.float32)


def _nsa_prompt_tables(rel_bias, t, tq):
    n_sub = t // CMP_STRIDE
    n_blk = -(-t // SEL_BLOCK)
    ov = jnp.asarray(_overlap_matrix(n_sub, n_blk), BF16)
    key_blk = np.arange(t) // SEL_BLOCK
    esel = (np.arange(n_blk)[:, None] == key_blk[None, :]).astype(np.float32)
    esel = jnp.asarray(esel.reshape(n_blk, t // tq, tq).transpose(1, 0, 2), BF16)
    qpos = np.arange(t)[:, None]
    dist_c = qpos - (np.arange(n_sub)[None, :] * CMP_STRIDE + CMP_BLOCK - 1)
    bias_c = jnp.transpose(rel_bias[_t5_bucket_np(dist_c)], (2, 0, 1)).astype(F32)
    i = np.arange(tq)[:, None]
    j = np.arange(tq)[None, :]
    dist_d = np.stack([i - j, tq + i - j])
    bias_d = jnp.transpose(rel_bias[_t5_bucket_np(dist_d)], (0, 3, 1, 2)).astype(F32)
    c31 = rel_bias[N_BUCKETS - 1].reshape(H_NSA, 1, 1).astype(F32)
    return ov, esel, bias_c, bias_d, c31


def _sample_paged_kernel(pt_ref, qabs_ref, mnew_ref, sbq_ref, sbnew_ref, nq_ref, wnew_ref, wbuf_ref,
                         cmla_ref, csb_ref, cnsa_ref,
                         wuv_ref, wc_ref, pe_ref, w2k_ref, w2v_ref, u_ref, ov_ref, gq_ref, bc_ref, bw1_ref, bw2_ref,
                         oa_ref, ob_ref, ocmp_ref, owin_ref, idx_ref,
                         mla_buf, sb_buf, nsa_buf, sem, m_sc, l_sc, acc_sc, r_sc, sbacc_sc, ab_sc,
                         *, layer, n_pg, n_sub, n_blk_pad, past, nq_tok):
    b = pl.program_id(0)
    j = pl.program_id(1)
    n_seq = pl.num_programs(0)
    n_step = pl.num_programs(1)
    d = HEAD_DIM
    c = MLA_KV_RANK

    def page_copies(bb, jj, slot):
        out = []
        for i in range(n_pg):
            pg = pt_ref[bb, (n_step - 1 - jj) * n_pg + i]
            rows = pl.ds(i * LANE, LANE)
            out.append(pltpu.make_async_copy(cmla_ref.at[layer, pg], mla_buf.at[slot, rows], sem.at[slot, 0]))
            out.append(pltpu.make_async_copy(csb_ref.at[layer, pg], sb_buf.at[slot, rows], sem.at[slot, 1]))
            out.append(pltpu.make_async_copy(cnsa_ref.at[layer, pg, :, pl.ds(0, 2 * d)], nsa_buf.at[slot, rows],
                                             sem.at[slot, 2]))
        return out

    step = b * n_step + j
    slot = step & 1

    @pl.when(step == 0)
    def _():
        for cp in page_copies(b, j, slot):
            cp.start()

    @pl.when(step + 1 < n_seq * n_step)
    def _():
        wrap = j + 1 == n_step
        for cp in page_copies(jnp.where(wrap, b + 1, b), jnp.where(wrap, 0, j + 1), 1 - slot):
            cp.start()

    for cp in page_copies(b, j, slot):
        cp.wait()
    q_mla = qabs_ref[...]
    q_sb = sbq_ref[...]
    u_tri = u_ref[...]
    row_q = lax.broadcasted_iota(jnp.int32, (ROWS_Q, 1), 0) >> 3
    row_q_sb = lax.broadcasted_iota(jnp.int32, (nq_tok * H_SB, 1), 0) >> 2
    new_col = lax.broadcasted_iota(jnp.int32, (1, NEW_PAD), 1)

    @pl.when(j == 0)
    def _():
        mnew = mnew_ref[...].astype(BF16)
        s = _dot_nt(q_mla, mnew) * MLA_SCALE
        p, _, m, l = _softmax_step(s, new_col <= row_q, jnp.full((ROWS_Q, 1), NEG_BIG, F32),
                                   jnp.zeros((ROWS_Q, 1), F32))
        m_sc[...] = m
        l_sc[...] = l
        acc_sc[...] = _dot(p.astype(BF16), mnew[:, :c])
        sbnew = sbnew_ref[...].astype(BF16)
        z = _dot_nt(q_sb, sbnew[:, :d]) * ATT_SCALE
        o, r_sum = _sb_tile(z, new_col < row_q_sb, jnp.zeros((nq_tok * H_SB, 1), F32), sbnew[:, d:],
                            u_tri[:NEW_PAD, :NEW_PAD])
        sbacc_sc[...] = o
        r_sc[...] = r_sum

    keys = mla_buf[slot].astype(BF16)
    s = _dot_nt(q_mla, keys) * MLA_SCALE
    m_old = m_sc[...]
    m_new = jnp.maximum(m_old, jnp.max(s, axis=1, keepdims=True))
    p = jnp.exp(s - m_new)
    alpha = jnp.exp(m_old - m_new)
    m_sc[...] = m_new
    l_sc[...] = alpha * l_sc[...] + jnp.sum(p, axis=1, keepdims=True)
    acc_sc[...] = alpha * acc_sc[...] + _dot(p.astype(BF16), keys[:, :c])

    r_sum = r_sc[...]
    sb_acc = sbacc_sc[...]
    for ch in reversed(range(n_pg * LANE // SB_CHUNK)):
        kv = sb_buf[slot, ch * SB_CHUNK:(ch + 1) * SB_CHUNK, :].astype(BF16)
        z = _dot_nt(q_sb, kv[:, :d]) * ATT_SCALE
        o, r_sum = _sb_tile(z, None, r_sum, kv[:, d:], u_tri)
        sb_acc = sb_acc + o
    r_sc[...] = r_sum
    sbacc_sc[...] = sb_acc

    ab = jnp.zeros((n_pg * 8, 4 * d), F32)
    for i2 in range(CMP_STRIDE // 2):
        lhs = jnp.concatenate([nsa_buf[slot, pl.ds(2 * i2, n_pg * 8, stride=CMP_STRIDE), :],
                               nsa_buf[slot, pl.ds(2 * i2 + 1, n_pg * 8, stride=CMP_STRIDE), :]], axis=1)
        ab = ab + _dot(lhs.astype(BF16), wc_ref[i2])
    sub0 = pl.multiple_of((n_step - 1 - j) * (n_pg * 8), n_pg * 8)
    ab_sc[pl.ds(sub0, n_pg * 8), :] = ab

    @pl.when(j == n_step - 1)
    def _():
        o_lat = (acc_sc[...] / jnp.maximum(l_sc[...], 1e-30)).astype(BF16)
        res = _dot(o_lat, wuv_ref[...])
        head = lax.broadcasted_iota(jnp.int32, (ROWS_Q, 1), 0) & 7
        oa = jnp.zeros((ROWS_Q, MLA_V), F32)
        for h in range(H_MLA):
            oa = oa + jnp.where(head == h, res[:, h * MLA_V:(h + 1) * MLA_V], 0.0)
        oa_ref[...] = oa
        ob_ref[...] = sbacc_sc[...]

        kc, vc = _compress_finish(ab_sc[...] + _compress_bias(pe_ref, wc_ref), w2k_ref, w2v_ref)
        q_n = nq_ref[...]
        c_idx = lax.broadcasted_iota(jnp.int32, (1, n_sub), 1)
        mask_c = (c_idx * CMP_STRIDE + (CMP_BLOCK - 1)) <= (past + row_q)
        s_c = jnp.where(mask_c, _dot_nt(q_n, kc) * ATT_SCALE + bc_ref[...], NEG_BIG)
        e = jnp.where(mask_c, jnp.exp(s_c - jnp.max(s_c, axis=1, keepdims=True)), 0.0)
        p_c = e / jnp.maximum(jnp.sum(e, axis=1, keepdims=True), 1e-30)
        ocmp_ref[...] = _dot(p_c.astype(BF16), vc)
        p_hi = p_c.astype(BF16)
        p_lo = (p_c - p_hi.astype(F32)).astype(BF16)
        psum = _dot(gq_ref[...], p_hi) + _dot(gq_ref[...], p_lo)
        imp = _dot_hilo(psum, ov_ref[...])

        blk = lax.broadcasted_iota(jnp.int32, (8, n_blk_pad), 1)
        blk_f = blk.astype(F32)
        cur = (past + lax.broadcasted_iota(jnp.int32, (8, 1), 0)) >> 6
        forced = (blk == 0) | (blk == cur) | (blk == cur - 1)
        score = jnp.where(blk <= cur, jnp.where(forced, FORCE_SCORE, imp), -1.0)
        lane = lax.broadcasted_iota(jnp.int32, (8, LANE), 1)
        picked = jnp.full((8, LANE), -1.0, F32)
        for t in range(SEL_TOP_N):
            best = jnp.max(score, axis=1, keepdims=True)
            arg = jnp.min(jnp.where(score == best, blk_f, float(n_blk_pad)), axis=1, keepdims=True)
            picked = jnp.where(lane == t, jnp.where(best >= 0.0, arg, -1.0), picked)
            score = jnp.where(blk_f == arg, -3e38, score)
        idx_ref[...] = picked.astype(jnp.int32)

        wbuf = wbuf_ref[...].astype(BF16)
        wnew = wnew_ref[...].astype(BF16)
        n_buf = wbuf.shape[0]
        buf_col = lax.broadcasted_iota(jnp.int32, (1, n_buf), 1)
        dist1 = n_buf + row_q - buf_col
        mask1 = (dist1 >= 0) & (dist1 < WINDOW)
        mask2 = new_col <= row_q
        s1 = jnp.where(mask1, _dot_nt(q_n, wbuf[:, :d]) * ATT_SCALE + bw1_ref[...], NEG_BIG)
        s2 = jnp.where(mask2, _dot_nt(q_n, wnew[:, :d]) * ATT_SCALE + bw2_ref[...], NEG_BIG)
        m_w = jnp.maximum(jnp.max(s1, axis=1, keepdims=True), jnp.max(s2, axis=1, keepdims=True))
        e1 = jnp.where(mask1, jnp.exp(s1 - m_w), 0.0)
        e2 = jnp.where(mask2, jnp.exp(s2 - m_w), 0.0)
        l_w = jnp.sum(e1, axis=1, keepdims=True) + jnp.sum(e2, axis=1, keepdims=True)
        o_w = _dot(e1.astype(BF16), wbuf[:, d:]) + _dot(e2.astype(BF16), wnew[:, d:])
        owin_ref[...] = o_w / jnp.maximum(l_w, 1e-30)


def _sample_paged(layer, page_table, qabs, mla_new, sbq, sb_new, nq, win_new, win_buf,
                  cache_mla, cache_sb, cache_nsa, consts, *, past):
    nb, n_pages = page_table.shape
    n_pg = min(PAGES_PER_STEP, n_pages)
    n_step = n_pages // n_pg
    n_sub = past // CMP_STRIDE
    nq_tok = sbq.shape[1] // H_SB
    wuv, wc, pe, w2k, w2v, u_tri, ov, gq, bias_c, bias_w1, bias_w2 = consts
    n_blk_pad = ov.shape[1]
    d = HEAD_DIM

    def seq(a):
        return pl.BlockSpec((None,) + a.shape[1:], lambda b, j, pt: (b,) + (0,) * (a.ndim - 1))

    cst = lambda a: pl.BlockSpec(a.shape, lambda b, j, pt: (0,) * a.ndim)
    seq_in = [qabs, mla_new, sbq, sb_new, nq, win_new, win_buf]
    const_in = [wuv, wc, pe, w2k, w2v, u_tri, ov, gq, bias_c, bias_w1, bias_w2]
    in_specs = ([seq(a) for a in seq_in] + [pl.BlockSpec(memory_space=pl.ANY)] * 3 + [cst(a) for a in const_in])
    out_shape = [jax.ShapeDtypeStruct((nb, ROWS_Q, MLA_V), F32),
                 jax.ShapeDtypeStruct((nb, nq_tok * H_SB, d), F32),
                 jax.ShapeDtypeStruct((nb, ROWS_Q, d), F32),
                 jax.ShapeDtypeStruct((nb, ROWS_Q, d), F32),
                 jax.ShapeDtypeStruct((nb, 8, LANE), jnp.int32)]
    out_specs = [pl.BlockSpec((None,) + s.shape[1:], lambda b, j, pt: (b, 0, 0)) for s in out_shape]
    kernel = functools.partial(_sample_paged_kernel, layer=layer, n_pg=n_pg, n_sub=n_sub, n_blk_pad=n_blk_pad,
                               past=past, nq_tok=nq_tok)
    return pl.pallas_call(
        kernel,
        grid_spec=pltpu.PrefetchScalarGridSpec(
            num_scalar_prefetch=1,
            grid=(nb, n_step),
            in_specs=in_specs,
            out_specs=out_specs,
            scratch_shapes=[pltpu.VMEM((2, n_pg * LANE, MLA_KV_RANK + MLA_ROPE), F32),
                            pltpu.VMEM((2, n_pg * LANE, 2 * d), F32),
                            pltpu.VMEM((2, n_pg * LANE, 2 * d), F32),
                            pltpu.SemaphoreType.DMA((2, 3)),
                            pltpu.VMEM((ROWS_Q, 1), F32), pltpu.VMEM((ROWS_Q, 1), F32),
                            pltpu.VMEM((ROWS_Q, MLA_KV_RANK), F32),
                            pltpu.VMEM((nq_tok * H_SB, 1), F32), pltpu.VMEM((nq_tok * H_SB, d), F32),
                            pltpu.VMEM((n_sub, 4 * d), F32)]),
        out_shape=out_shape,
        compiler_params=_cparams(("arbitrary", "arbitrary")),
        name="sample_paged",
    )(page_table, *seq_in, cache_mla, cache_sb, cache_nsa, *const_in)


def _sample_slc_kernel(idx_ref, pt_ref, nq_ref, nnew_ref, ocmp_ref, owin_ref, gate_ref, tsp_ref, c31_ref, cache_ref,
                       o_ref, kv_buf, sem, *, layer, past, nq_tok):
    b = pl.program_id(0)
    d = HEAD_DIM
    n_past_blk = past // SEL_BLOCK
    blk_per_page = LANE // SEL_BLOCK

    def copies(q):
        out = []
        for k in range(SEL_TOP_N):
            blk = jnp.clip(idx_ref[b, q, k], 0, n_past_blk - 1)
            page = pt_ref[b, blk // blk_per_page]
            r0 = pl.multiple_of((blk % blk_per_page) * SEL_BLOCK, SEL_BLOCK)
            out.append(pltpu.make_async_copy(
                cache_ref.at[layer, page, pl.ds(r0, SEL_BLOCK), pl.ds(2 * d, 2 * d)], kv_buf.at[q, k], sem.at[q]))
        return out

    for q in range(nq_tok):
        for cp in copies(q):
            cp.start()

    lane = lax.broadcasted_iota(jnp.int32, (1, SEL_TOP_N * SEL_BLOCK), 1)
    slot = lane >> 6
    new_col = lax.broadcasted_iota(jnp.int32, (1, NEW_PAD), 1)
    nnew = nnew_ref[...].astype(BF16)
    c31 = c31_ref[...]
    for q in range(nq_tok):
        for cp in copies(q):
            cp.wait()
        idx_vec = jnp.full(lane.shape, -1, jnp.int32)
        for k in range(SEL_TOP_N):
            idx_vec = jnp.where(slot == k, idx_ref[b, q, k], idx_vec)
        has_new = jnp.max((idx_vec == n_past_blk).astype(jnp.int32), axis=1, keepdims=True) > 0
        kpos = idx_vec * SEL_BLOCK + (lane & (SEL_BLOCK - 1))
        mask = (idx_vec >= 0) & (idx_vec < n_past_blk) & (kpos <= past + q)
        rows = slice(8 * q, 8 * q + 8)
        bias = jnp.broadcast_to(c31[rows, 0:1], (8, SEL_TOP_N * SEL_BLOCK))
        for t in (1, 2):
            bias = jnp.where(idx_vec == n_past_blk - t, tsp_ref[t, rows, :], bias)
        kv = kv_buf[q].reshape(SEL_TOP_N * SEL_BLOCK, 2 * d).astype(BF16)
        qn = nq_ref[rows, :].astype(BF16)
        s1 = jnp.where(mask, _dot_nt(qn, kv[:, :d]) * ATT_SCALE + bias, NEG_BIG)
        mask2 = (new_col <= q) & (new_col < nq_tok) & has_new
        s2 = jnp.where(mask2, _dot_nt(qn, nnew[:, 2 * d:3 * d]) * ATT_SCALE + tsp_ref[0, rows, 0:NEW_PAD], NEG_BIG)
        m = jnp.maximum(jnp.max(s1, axis=1, keepdims=True), jnp.max(s2, axis=1, keepdims=True))
        e1 = jnp.where(mask, jnp.exp(s1 - m), 0.0)
        e2 = jnp.where(mask2, jnp.exp(s2 - m), 0.0)
        l = jnp.sum(e1, axis=1, keepdims=True) + jnp.sum(e2, axis=1, keepdims=True)
        o_slc = (_dot(e1.astype(BF16), kv[:, d:]) + _dot(e2.astype(BF16), nnew[:, 3 * d:])) / jnp.maximum(l, 1e-30)
        g = gate_ref[rows, :]
        o_ref[rows, :] = g[:, 0:1] * ocmp_ref[rows, :] + g[:, 1:2] * o_slc + g[:, 2:3] * owin_ref[rows, :]


def _sample_slc(layer, idx, page_table, nq32, nsa_new, o_cmp, o_win, gates32, tsp, c31, cache_nsa, *, past):
    nb = page_table.shape[0]
    nq_tok = idx.shape[1]
    d = HEAD_DIM
    seq = lambda a: pl.BlockSpec((None,) + a.shape[1:], lambda b, i, p: (b,) + (0,) * (a.ndim - 1))
    cst = lambda a: pl.BlockSpec(a.shape, lambda b, i, p: (0,) * a.ndim)
    return pl.pallas_call(
        functools.partial(_sample_slc_kernel, layer=layer, past=past, nq_tok=nq_tok),
        grid_spec=pltpu.PrefetchScalarGridSpec(
            num_scalar_prefetch=2,
            grid=(nb,),
            in_specs=[seq(nq32), seq(nsa_new), seq(o_cmp), seq(o_win), seq(gates32), cst(tsp), cst(c31),
                      pl.BlockSpec(memory_space=pl.ANY)],
            out_specs=pl.BlockSpec((None, ROWS_Q, d), lambda b, i, p: (b, 0, 0)),
            scratch_shapes=[pltpu.VMEM((nq_tok, SEL_TOP_N, SEL_BLOCK, 2 * d), F32),
                            pltpu.SemaphoreType.DMA((nq_tok,))]),
        out_shape=jax.ShapeDtypeStruct((nb, ROWS_Q, d), F32),
        compiler_params=_cparams(("arbitrary",)),
        name="sample_slc",
    )(idx, page_table, nq32, nsa_new, o_cmp, o_win, gates32, tsp, c31, cache_nsa)


def _sample_tables(rel_bias, past, nq_tok, n_buf):
    n_sub = past // CMP_STRIDE
    n_blk = -(-(past + nq_tok) // SEL_BLOCK)
    n_blk_pad = -(-n_blk // LANE) * LANE
    ov = jnp.asarray(_overlap_matrix(n_sub, n_blk_pad), BF16)
    gq = np.zeros((8, ROWS_Q), np.float32)
    for q in range(nq_tok):
        gq[q, 8 * q:8 * q + H_NSA] = 1.0
    qpos = past + np.arange(nq_tok)[:, None]
    bias_c = _bias_rows(rel_bias, qpos - (np.arange(n_sub)[None, :] * CMP_STRIDE + CMP_BLOCK - 1), 8)
    bias_w1 = _bias_rows(rel_bias, qpos - (past - n_buf + np.arange(n_buf)[None, :]), 8)
    bias_w2 = _bias_rows(rel_bias, qpos - (past + np.arange(NEW_PAD)[None, :]), 8)
    s = np.arange(SEL_BLOCK)[None, :]
    tsp = jnp.stack([jnp.tile(_bias_rows(rel_bias, qpos - (past - t * SEL_BLOCK + s), 8), (1, SEL_TOP_N))
                     for t in range(3)])
    c31 = jnp.pad(jnp.broadcast_to(rel_bias[N_BUCKETS - 1][None, :, None], (nq_tok, H_NSA, LANE)),
                  ((0, 0), (0, 8 - H_NSA), (0, 0))).reshape(ROWS_Q, LANE).astype(F32)
    return ov, jnp.asarray(gq, BF16), bias_c, bias_w1, bias_w2, tsp, c31


def _rows_q8(a, nb, nq_tok, h):
    w = a.shape[1] // h
    a = a.reshape(nb, nq_tok, h, w)
    a = jnp.pad(a, ((0, 0), (0, 0), (0, 8 - h), (0, 0)))
    return a.reshape(nb, nq_tok * 8, w)


def _pad_new(a, nb, nq_tok):
    a = a.reshape(nb, nq_tok, a.shape[-1])
    return jnp.pad(a, ((0, 0), (0, NEW_PAD - nq_tok), (0, 0)))


def kernel(x_prompt, x_sample, cache_mla, cache_sb, cache_nsa, state_win, page_table, p_prompt, p_sample, g_attn, w_in, g_cq, g_ckv, w_uq, w_uk, w_uv, cmp_pe_k, cmp_w1_k, cmp_w2_k, cmp_pe_v, cmp_w1_v, cmp_w2_v, rel_bias, g_grp_mla, g_grp_sb, g_grp_nsa, w_out, g_ffn, w_gate, w_up, w_down, g_ple, w_ple_gate, w_ple, g_final):
    bsz, t, dm = x_prompt.shape
    nb, nq_tok, _ = x_sample.shape
    depth = w_in.shape[0]
    n_pages = page_table.shape[1]
    past = n_pages * cache_mla.shape[2]
    n_buf = state_win.shape[2]
    assert cache_mla.shape[2] == LANE and nq_tok * 8 == ROWS_Q and SEL_BLOCK == 64
    assert past % (PAGES_PER_STEP * LANE) == 0 or n_pages < PAGES_PER_STEP
    tq = min(256, t)
    tm_p = min(512, bsz * t)
    tm_s = min(256, nb * nq_tok)
    win_keep = min(WINDOW, t)
    d = HEAD_DIM

    tab_p = _rope_tables(jnp.arange(t, dtype=jnp.int32))
    tab_s = _rope_tables(jnp.tile(past + jnp.arange(nq_tok, dtype=jnp.int32), tm_s // nq_tok))
    u_tri = jnp.asarray(np.tril(np.ones((SB_CHUNK, SB_CHUNK), np.float32), -1), BF16)
    ptabs = _nsa_prompt_tables(rel_bias, t, tq)
    ov_s, gq, bias_c, bias_w1, bias_w2, tsp, c31_s = _sample_tables(rel_bias, past, nq_tok, n_buf)

    xp = x_prompt.reshape(bsz * t, dm)
    xs = x_sample.reshape(nb * nq_tok, dm)
    outs = [[] for _ in range(8)]
    for i in range(depth):
        final = i == depth - 1
        pw = _proj_weights(w_in[i], w_uq[i], w_uk[i], w_uv[i])
        fw = _finish_weights(g_grp_mla[i], g_grp_sb[i], g_grp_nsa[i], w_out[i], g_ffn[i], w_gate[i], w_up[i],
                             w_down[i], g_ple[i], w_ple_gate[i], w_ple[i], g_final)
        wc, pe = _compress_weights(cmp_pe_k[i], cmp_w1_k[i], cmp_pe_v[i], cmp_w1_v[i])
        w2k, w2v = cmp_w2_k[i].astype(BF16), cmp_w2_v[i].astype(BF16)

        (mla_r, sb_r, nsa_r, win_r, gates, sbq, nq, qa, ka, va) = _proj(
            xp, tab_p, pw, g_attn[i], g_cq[i], g_ckv[i], prompt=True, tm=tm_p, table_period=t)
        o_a = _mla_prompt(qa, ka, va, b=bsz, t=t, tq=tq)
        o_b = _sb_prompt(sbq, sb_r, u_tri[:tq, :tq], b=bsz, t=t, tq=tq)
        o_c = _nsa_prompt(nq, nsa_r, win_r, gates, (wc, pe, w2k, w2v), ptabs, b=bsz, t=t, tq=tq)
        xp = _finish(xp, o_a, o_b, o_c, p_prompt[i].reshape(bsz * t, -1), fw, final=final, tm=tm_p)
        outs[0].append(mla_r.reshape(bsz, t, -1))
        outs[2].append(sb_r.reshape(bsz, t, -1))
        outs[4].append(nsa_r.reshape(bsz, t, -1))
        outs[6].append(win_r.reshape(bsz, t, -1)[:, t - win_keep:])

        (mla_n, sb_n, nsa_n, win_n, gates_s, sbq_s, nq_s, qa_s, qlat_s) = _proj(
            xs, tab_s, pw, g_attn[i], g_cq[i], g_ckv[i], prompt=False, tm=tm_s, table_period=tm_s)
        q_pe = qa_s.reshape(nb * nq_tok, H_MLA, SLOT)[:, :, MLA_NOPE:MLA_NOPE + MLA_ROPE]
        qabs = jnp.concatenate([qlat_s.reshape(nb * nq_tok, H_MLA, MLA_KV_RANK), q_pe], axis=2)
        qabs = _rows_q8(qabs.reshape(nb * nq_tok, -1), nb, nq_tok, H_MLA)
        sbq3 = sbq_s.reshape(nb, nq_tok * H_SB, d)
        nq32 = _rows_q8(nq_s, nb, nq_tok, H_NSA)
        consts = (w_uv[i].reshape(MLA_KV_RANK, -1).astype(BF16), wc, pe, w2k, w2v, u_tri, ov_s, gq, bias_c,
                  bias_w1, bias_w2)
        o_a8, o_b4, o_cmp, o_win, idx = _sample_paged(
            i, page_table, qabs, _pad_new(mla_n, nb, nq_tok), sbq3, _pad_new(sb_n, nb, nq_tok), nq32,
            _pad_new(win_n, nb, nq_tok), state_win[i], cache_mla, cache_sb, cache_nsa, consts, past=past)
        gates32 = _rows_q8(gates_s[:, :3 * H_NSA], nb, nq_tok, H_NSA)
        gates32 = jnp.pad(gates32, ((0, 0), (0, 0), (0, LANE - 3)))
        o_c8 = _sample_slc(i, idx[:, :nq_tok, :SEL_TOP_N], page_table, nq32.astype(F32),
                           _pad_new(nsa_n, nb, nq_tok), o_cmp, o_win, gates32, tsp, c31_s, cache_nsa, past=past)
        unrow = lambda a, h: a.reshape(nb, nq_tok, 8, -1)[:, :, :h].reshape(nb * nq_tok, -1)
        xs = _finish(xs, unrow(o_a8, H_MLA), o_b4.reshape(nb * nq_tok, -1), unrow(o_c8, H_NSA),
                     p_sample[i].reshape(nb * nq_tok, -1), fw, final=final, tm=tm_s)
        outs[1].append(mla_n.reshape(nb, nq_tok, -1))
        outs[3].append(sb_n.reshape(nb, nq_tok, -1))
        outs[5].append(nsa_n.reshape(nb, nq_tok, -1))
        win_all = jnp.concatenate([state_win[i], win_n.reshape(nb, nq_tok, -1)], axis=1)
        outs[7].append(win_all[:, nq_tok:])

    y_prompt = xp.reshape(bsz, t, dm)
    y_sample = xs.reshape(nb, nq_tok, dm)
    st = [jnp.stack(o) for o in outs]
    return (y_prompt, y_sample, st[0], st[1], st[2], st[3], st[4], st[5], st[6], st[7])
```

```python
import functools
import math

import numpy as np
import jax
import jax.numpy as jnp
from jax import lax
from jax.experimental import pallas as pl
from jax.experimental.pallas import tpu as pltpu

F32 = jnp.float32
BF16 = jnp.bfloat16

HEAD_DIM = 64
H_MLA = 6
H_SB = 4
H_NSA = 6
MLA_Q_RANK = 256
MLA_KV_RANK = 256
MLA_NOPE = 64
MLA_ROPE = 32
MLA_V = 64
MLA_SCALE = (MLA_NOPE + MLA_ROPE) ** -0.5
ATT_SCALE = HEAD_DIM ** -0.5
ROPE_BASE = 10000.0
CMP_BLOCK = 32
CMP_STRIDE = 16
SEL_BLOCK = 64
SEL_TOP_N = 16
FORCE_SCORE = 1e4
WINDOW = 512
N_BUCKETS = 32
MAX_DISTANCE = 128
EPS = 1e-6
NEG_BIG = -1e30
IN_SPLITS = [MLA_Q_RANK, MLA_KV_RANK, MLA_ROPE, H_SB * HEAD_DIM, 2 * HEAD_DIM,
             H_NSA * HEAD_DIM, 4 * HEAD_DIM, 2 * HEAD_DIM, 3 * H_NSA]

LANE = 128
SLOT = 128
ROPE_HALF = MLA_ROPE // 2
VMEM_LIMIT = 56 * 1024 * 1024
PAGES_PER_STEP = 32
SB_CHUNK = 256
ROWS_Q = 32
NEW_PAD = 16


def _cparams(sem):
    return pltpu.CompilerParams(dimension_semantics=sem, vmem_limit_bytes=VMEM_LIMIT)


def _rms(x, g):
    return x * lax.rsqrt(jnp.mean(x * x, axis=-1, keepdims=True) + EPS) * g


def _dot(a, b):
    return jnp.dot(a, b, preferred_element_type=F32)


def _dot_nt(a, b):
    return lax.dot_general(a, b, (((1,), (1,)), ((), ())), preferred_element_type=F32)


def _dot_hilo(x, w):
    hi = x.astype(BF16)
    lo = (x - hi.astype(F32)).astype(BF16)
    return _dot(hi, w) + _dot(lo, w)


def _gelu_tanh(x):
    return 0.5 * x * (1.0 + jnp.tanh(math.sqrt(2.0 / math.pi) * (x + 0.044715 * (x * x * x))))


def _full(shape):
    n = len(shape)
    return pl.BlockSpec(shape, lambda *_: (0,) * n)


def _t5_bucket_np(dist):
    n = np.maximum(dist, 0)
    max_exact = N_BUCKETS // 2
    nf = np.maximum(n, 1).astype(np.float32)
    large = max_exact + (np.log(nf / np.float32(max_exact)) / np.float32(math.log(MAX_DISTANCE / max_exact))
                         * np.float32(N_BUCKETS - max_exact)).astype(np.int32)
    large = np.minimum(large, N_BUCKETS - 1)
    return np.where(n < max_exact, n, large).astype(np.int32)


def _bias_lookup(rel_bias, dist):
    bucket = jnp.asarray(_t5_bucket_np(dist))
    one_hot = (bucket[..., None] == jnp.arange(N_BUCKETS, dtype=jnp.int32)).astype(F32)
    return jnp.dot(one_hot, rel_bias.astype(F32), precision=lax.Precision.HIGHEST)


def _bias_rows(rel_bias, dist, rows_per_q):
    nq, k = dist.shape
    b = _bias_lookup(rel_bias, dist)
    b = jnp.transpose(b, (0, 2, 1))
    b = jnp.pad(b, ((0, 0), (0, rows_per_q - H_NSA), (0, 0)))
    return b.reshape(nq * rows_per_q, k).astype(F32)


def _rope_tables(pos):
    freqs = ROPE_BASE ** (-jnp.arange(ROPE_HALF, dtype=F32) / ROPE_HALF)
    ang = pos.astype(F32)[:, None] * freqs[None, :]
    cos, sin = jnp.cos(ang), jnp.sin(ang)
    n = pos.shape[0]
    one = jnp.ones((n, MLA_NOPE), F32)
    zq = jnp.zeros((n, SLOT - MLA_NOPE - MLA_ROPE), F32)
    cq = jnp.concatenate([one, cos, cos, zq], axis=1)
    sq = jnp.concatenate([0 * one, sin, sin, zq], axis=1)
    zk = jnp.zeros((n, LANE - MLA_ROPE), F32)
    ck = jnp.concatenate([cos, cos, zk], axis=1)
    sk = jnp.concatenate([sin, sin, zk], axis=1)
    return cq, sq, ck, sk


U_CQ, U_CKV, U_SBQ, U_SBR, U_NQ, U_NR, U_WIN, U_GATE, U_KPE, U_KPES, U_END = (
    0, 256, 512, 768, 896, 1280, 1536, 1664, 1792, 1920, 2048)


def _proj_weights(w_in, w_uq, w_uk, w_uv):
    cuts = [int(c) for c in np.cumsum(IN_SPLITS[:-1])]
    cq, ckv, kpe, sbq, sbr, nq, nr, wr, gt = jnp.split(w_in, cuts, axis=1)
    pad = lambda w, n: jnp.pad(w, ((0, 0), (0, n - w.shape[1])))
    kpe_sw = jnp.concatenate([-kpe[:, ROPE_HALF:], kpe[:, :ROPE_HALF]], axis=1)
    w_all = jnp.concatenate([cq, ckv, sbq, sbr, nq, nr, wr, pad(gt, LANE), pad(kpe, LANE), pad(kpe_sw, LANE)],
                            axis=1).astype(BF16)
    r = w_uq.shape[0]
    zpad = jnp.zeros((r, H_MLA, SLOT - MLA_NOPE - MLA_ROPE), F32)
    wq = jnp.concatenate([w_uq, zpad], axis=2).reshape(r, H_MLA * SLOT).astype(BF16)
    pe1 = w_uq[:, :, MLA_NOPE:MLA_NOPE + ROPE_HALF]
    pe2 = w_uq[:, :, MLA_NOPE + ROPE_HALF:]
    wqs = jnp.concatenate([jnp.zeros((r, H_MLA, MLA_NOPE), F32), -pe2, pe1, zpad], axis=2)
    wqs = wqs.reshape(r, H_MLA * SLOT).astype(BF16)
    c = w_uk.shape[0]
    wk = jnp.concatenate([w_uk, jnp.zeros((c, H_MLA, SLOT - MLA_NOPE), F32)], axis=2)
    wk = wk.reshape(c, H_MLA * SLOT).astype(BF16)
    wv = w_uv.reshape(c, H_MLA * MLA_V).astype(BF16)
    wukt = jnp.transpose(w_uk, (1, 2, 0))
    wukt = jnp.pad(wukt, ((0, 0), (0, SLOT - MLA_NOPE), (0, 0)))
    eye = jnp.eye(H_MLA, dtype=F32)
    wukbd = (eye[:, None, :, None] * wukt[:, :, None, :]).reshape(H_MLA * SLOT, H_MLA * c).astype(BF16)
    return w_all, wq, wqs, wk, wv, wukbd


def _proj_kernel(*refs, prompt, tq):
    (x_ref, gat_ref, wall_ref, gcq_ref, gckv_ref, wq_ref, wqs_ref, cq_ref, sq_ref, ck_ref, sk_ref) = refs[:11]
    if prompt:
        wk_ref, wv_ref = refs[11:13]
        mla_o, sb_o, nsa_o, win_o, gate_o, sbq_o, nq_o, qa_o, ka_o, va_o, sbt_o, slct_o, wint_o = refs[13:]
    else:
        (wukbd_ref,) = refs[11:12]
        mla_o, sb_o, nsa_o, win_o, gate_o, sbq_o, nq_o, qa_o, qlat_o = refs[12:]

    def put(o_ref, val):
        if not prompt:
            o_ref[...] = val.astype(o_ref.dtype)
            return
        for c in range(o_ref.shape[0]):
            o_ref[c] = val[c * tq:(c + 1) * tq, :].T.astype(o_ref.dtype)

    xn = _rms(x_ref[...], gat_ref[...]).astype(BF16)
    u = _dot(xn, wall_ref[...])
    put(sbq_o, u[:, U_SBQ:U_SBR])
    sb_o[...] = u[:, U_SBR:U_NQ]
    put(nq_o, u[:, U_NQ:U_NR])
    nsa_o[...] = u[:, U_NR:U_WIN]
    win_o[...] = u[:, U_WIN:U_GATE]
    put(gate_o, jax.nn.sigmoid(u[:, U_GATE:U_KPE]))
    if prompt:
        put(sbt_o, u[:, U_SBR:U_NQ])
        put(slct_o, u[:, U_NR + 2 * HEAD_DIM:U_WIN])
        put(wint_o, u[:, U_WIN:U_GATE])
    kr = u[:, U_KPE:U_KPES] * ck_ref[...] + u[:, U_KPES:U_END] * sk_ref[...]
    ckvn = _rms(u[:, U_CKV:U_SBQ], gckv_ref[...])
    mla_o[:, 0:MLA_KV_RANK] = ckvn
    mla_o[:, MLA_KV_RANK:MLA_KV_RANK + MLA_ROPE] = kr[:, 0:MLA_ROPE]
    cqn = _rms(u[:, U_CQ:U_CKV], gcq_ref[...]).astype(BF16)
    qf = _dot(cqn, wq_ref[...])
    qs = _dot(cqn, wqs_ref[...])
    cq_t, sq_t = cq_ref[...], sq_ref[...]
    qrot = jnp.concatenate([qf[:, h * SLOT:(h + 1) * SLOT] * cq_t + qs[:, h * SLOT:(h + 1) * SLOT] * sq_t
                            for h in range(H_MLA)], axis=1)
    put(qa_o, qrot)
    if prompt:
        ckb = ckvn.astype(BF16)
        kn = _dot(ckb, wk_ref[...])
        krr = pltpu.roll(kr, MLA_NOPE, axis=1)
        for h in range(H_MLA):
            sl = slice(h * SLOT, (h + 1) * SLOT)
            ka_o[:, sl] = (kn[:, sl] + krr).astype(BF16)
        put(va_o, _dot(ckb, wv_ref[...]))
    else:
        qlat_o[...] = _dot(qrot.astype(BF16), wukbd_ref[...]).astype(BF16)


def _proj(x, tables, pw, g_attn, g_cq, g_ckv, *, prompt, tm, table_period, tq=None):
    n, d = x.shape
    w_all, wq, wqs, wk, wv, wukbd = pw
    cq_t, sq_t, ck_t, sk_t = tables
    nt = table_period // tm
    row = lambda w: pl.BlockSpec((tm, w), lambda i: (i, 0))
    tab = pl.BlockSpec((tm, LANE), lambda i: (i % nt, 0))
    ins = [x, g_attn.reshape(1, d), w_all, g_cq.reshape(1, -1), g_ckv.reshape(1, -1), wq, wqs, cq_t, sq_t, ck_t, sk_t]
    in_specs = [row(d), _full((1, d)), _full(w_all.shape), _full((1, MLA_Q_RANK)), _full((1, MLA_KV_RANK)),
                _full(wq.shape), _full(wqs.shape), tab, tab, tab, tab]
    outs = [(MLA_KV_RANK + MLA_ROPE, F32, False), (2 * HEAD_DIM, F32, False), (4 * HEAD_DIM, F32, False),
            (2 * HEAD_DIM, F32, False), (LANE, F32, True), (H_SB * HEAD_DIM, BF16, True),
            (H_NSA * HEAD_DIM, BF16, True), (H_MLA * SLOT, BF16, True)]
    if prompt:
        ins += [wk, wv]
        in_specs += [_full(wk.shape), _full(wv.shape)]
        outs += [(H_MLA * SLOT, BF16, False), (H_MLA * MLA_V, BF16, True)] + [(2 * HEAD_DIM, BF16, True)] * 3
    else:
        ins += [wukbd]
        in_specs += [_full(wukbd.shape)]
        outs += [(H_MLA * MLA_KV_RANK, BF16, False)]
    col = lambda w: pl.BlockSpec((tm // tq, w, tq), lambda i: (i, 0, 0))
    return pl.pallas_call(
        functools.partial(_proj_kernel, prompt=prompt, tq=tq),
        grid=(n // tm,),
        in_specs=in_specs,
        out_specs=[col(w) if (prompt and tr) else row(w) for w, _, tr in outs],
        out_shape=[jax.ShapeDtypeStruct((n // tq, w, tq) if (prompt and tr) else (n, w), dt) for w, dt, tr in outs],
        compiler_params=_cparams(("parallel",)),
        name="proj_prompt" if prompt else "proj_sample",
    )(*ins)


def _finish_kernel(x_ref, oa_ref, ob_ref, oc_ref, p_ref, ga_ref, gb_ref, gc_ref, woa_ref, wob_ref, woc_ref,
                   gffn_ref, wg_ref, wu_ref, wd_ref, gple_ref, wpg_ref, wple_ref, gfin_ref, out_ref,
                   *, final, n_chunk):
    mix = (_dot(_rms(oa_ref[...], ga_ref[...]).astype(BF16), woa_ref[...])
           + _dot(_rms(ob_ref[...], gb_ref[...]).astype(BF16), wob_ref[...])
           + _dot(_rms(oc_ref[...], gc_ref[...]).astype(BF16), woc_ref[...]))
    x1 = x_ref[...] + mix
    h = _rms(x1, gffn_ref[...]).astype(BF16)
    fc = wg_ref.shape[1] // n_chunk
    ff = jnp.zeros_like(x1)
    for c in range(n_chunk):
        g = _dot(h, wg_ref[:, c * fc:(c + 1) * fc])
        u = _dot(h, wu_ref[:, c * fc:(c + 1) * fc])
        ff = ff + _dot((g * jax.nn.sigmoid(g) * u).astype(BF16), wd_ref[c * fc:(c + 1) * fc, :])
    x2 = x1 + ff
    pg = jax.nn.sigmoid(_dot(_rms(x2, gple_ref[...]).astype(BF16), wpg_ref[...]))
    x3 = x2 + pg * _dot(p_ref[...].astype(BF16), wple_ref[...])
    out_ref[...] = _rms(x3, gfin_ref[...]) if final else x3


def _finish(x, oa, ob, oc, p, fw, *, final, tm):
    n, d = x.shape
    (ga, gb, gc, woa, wob, woc, gffn, wg, wu, wd, gple, wpg, wple, gfin) = fw
    row = lambda w: pl.BlockSpec((tm, w), lambda i: (i, 0))
    const = lambda a: pl.BlockSpec(a.shape, lambda i: (0,) * a.ndim, pipeline_mode=pl.Buffered(1))
    ws = [ga, gb, gc, woa, wob, woc, gffn, wg, wu, wd, gple, wpg, wple, gfin]
    d_ff = wg.shape[1]
    n_chunk = 2 if d_ff % (2 * LANE) == 0 else 1
    return pl.pallas_call(
        functools.partial(_finish_kernel, final=final, n_chunk=n_chunk),
        grid=(n // tm,),
        in_specs=[row(d), row(oa.shape[1]), row(ob.shape[1]), row(oc.shape[1]), row(p.shape[1])] + [const(w) for w in ws],
        out_specs=row(d),
        out_shape=jax.ShapeDtypeStruct((n, d), F32),
        compiler_params=_cparams(("parallel",)),
        name="finish",
    )(x, oa, ob, oc, p, *ws)


def _finish_weights(g_grp_mla, g_grp_sb, g_grp_nsa, w_out, g_ffn, w_gate, w_up, w_down, g_ple, w_ple_gate, w_ple,
                    g_final):
    na, nb = H_MLA * MLA_V, H_SB * HEAD_DIM
    r = lambda g: g.reshape(1, -1)
    return (r(g_grp_mla), r(g_grp_sb), r(g_grp_nsa), w_out[:na].astype(BF16), w_out[na:na + nb].astype(BF16),
            w_out[na + nb:].astype(BF16), r(g_ffn), w_gate.astype(BF16), w_up.astype(BF16), w_down.astype(BF16),
            r(g_ple), w_ple_gate.astype(BF16), w_ple.astype(BF16), r(g_final))


def _softmax_step(s, mask, m, l):
    if mask is not None:
        s = jnp.where(mask, s, NEG_BIG)
    m_new = jnp.maximum(m, jnp.max(s, axis=1, keepdims=True))
    p = jnp.exp(s - m_new)
    if mask is not None:
        p = jnp.where(mask, p, 0.0)
    alpha = jnp.exp(m - m_new)
    return p, alpha, m_new, alpha * l + jnp.sum(p, axis=1, keepdims=True)


def _softmax_init(m_sc, l_sc, acc_sc):
    m_sc[...] = jnp.full(m_sc.shape, NEG_BIG, F32)
    l_sc[...] = jnp.zeros(l_sc.shape, F32)
    acc_sc[...] = jnp.zeros(acc_sc.shape, F32)


def _softmax_update_t(ss, mask, v_ts, m_sc, l_sc, acc_sc):
    heads = range(len(ss))
    if mask is not None:
        ss = [jnp.where(mask, s, NEG_BIG) for s in ss]
    m_old = [m_sc[h] for h in heads]
    m_new = [jnp.maximum(m_old[h], jnp.max(ss[h], axis=0, keepdims=True)) for h in heads]
    ps = [jnp.exp(ss[h] - m_new[h]) for h in heads]
    if mask is not None:
        ps = [jnp.where(mask, p, 0.0) for p in ps]
    alpha = [jnp.exp(m_old[h] - m_new[h]) for h in heads]
    pv = [_dot(v_ts[h], ps[h].astype(BF16)) for h in heads]
    for h in heads:
        m_sc[h] = m_new[h]
        l_sc[h] = alpha[h] * l_sc[h] + jnp.sum(ps[h], axis=0, keepdims=True)
        acc_sc[h] = alpha[h] * acc_sc[h] + pv[h]


def _softmax_result_t(l_sc, acc_sc, n_head):
    o_t = jnp.concatenate([acc_sc[h] / jnp.maximum(l_sc[h], 1e-30) for h in range(n_head)], axis=0)
    return o_t.T


def _tile_iotas(tq):
    key = lax.broadcasted_iota(jnp.int32, (tq, 1), 0)
    qry = lax.broadcasted_iota(jnp.int32, (1, tq), 1)
    return key, qry


def _mla_prompt_kernel(qt_ref, k_ref, vt_ref, o_ref, m_sc, l_sc, acc_sc, *, tq):
    qi = pl.program_id(1)
    _softmax_init(m_sc, l_sc, acc_sc)

    def tile(kt, mask):
        k0 = pl.multiple_of(kt * tq, tq)
        ss = [_dot(k_ref[pl.ds(k0, tq), h * SLOT:(h + 1) * SLOT], qt_ref[h * SLOT:(h + 1) * SLOT, :]) * MLA_SCALE
              for h in range(H_MLA)]
        v_ts = [vt_ref[kt, h * MLA_V:(h + 1) * MLA_V, :] for h in range(H_MLA)]
        _softmax_update_t(ss, mask, v_ts, m_sc, l_sc, acc_sc)

    def body(kt, carry):
        tile(kt, None)
        return carry

    lax.fori_loop(0, qi, body, 0)
    key, qry = _tile_iotas(tq)
    tile(qi, key <= qry)
    o_ref[...] = _softmax_result_t(l_sc, acc_sc, H_MLA)


def _mla_prompt(qa_t, ka, va_t, *, b, t, tq):
    nq = t // tq
    return pl.pallas_call(
        functools.partial(_mla_prompt_kernel, tq=tq),
        grid=(b, nq),
        in_specs=[pl.BlockSpec((None, H_MLA * SLOT, tq), lambda i, j: (i * nq + j, 0, 0)),
                  pl.BlockSpec((t, H_MLA * SLOT), lambda i, j: (i, 0)),
                  pl.BlockSpec((nq, H_MLA * MLA_V, tq), lambda i, j: (i, 0, 0))],
        out_specs=pl.BlockSpec((tq, H_MLA * MLA_V), lambda i, j: (i * nq + j, 0)),
        out_shape=jax.ShapeDtypeStruct((b * t, H_MLA * MLA_V), F32),
        scratch_shapes=[pltpu.VMEM((H_MLA, 1, tq), F32), pltpu.VMEM((H_MLA, 1, tq), F32),
                        pltpu.VMEM((H_MLA, MLA_V, tq), F32)],
        compiler_params=_cparams(("parallel", "arbitrary")),
        name="mla_prompt",
    )(qa_t, ka, va_t)


def _sb_tile(z, mask, r_sum, v, u_tri):
    sp = jnp.log(1.0 + jnp.exp(-jnp.abs(z)))
    log_keep = -jnp.maximum(z, 0.0) - sp
    log_beta = jnp.minimum(z, 0.0) - sp
    if mask is not None:
        log_keep = jnp.where(mask, log_keep, 0.0)
    excl = _dot_hilo(log_keep, u_tri)
    a = jnp.exp(log_beta + excl + r_sum)
    if mask is not None:
        a = jnp.where(mask, a, 0.0)
    return _dot(a.astype(BF16), v), r_sum + jnp.sum(log_keep, axis=1, keepdims=True)


def _sb_prompt_kernel(qt_ref, k_ref, kvt_ref, ut_ref, o_ref, r_sc, acc_sc, *, tq):
    qi = pl.program_id(1)
    d = HEAD_DIM
    r_sc[...] = jnp.zeros(r_sc.shape, F32)
    acc_sc[...] = jnp.zeros(acc_sc.shape, F32)
    q_t = jnp.concatenate([qt_ref[h * d:(h + 1) * d, :] for h in range(H_SB)], axis=1)
    ut = ut_ref[...]

    def tile(kt, mask):
        k0 = pl.multiple_of(kt * tq, tq)
        z = _dot(k_ref[pl.ds(k0, tq), 0:d].astype(BF16), q_t) * ATT_SCALE
        sp = jnp.log(1.0 + jnp.exp(-jnp.abs(z)))
        log_keep = -jnp.maximum(z, 0.0) - sp
        if mask is not None:
            log_keep = jnp.where(mask, log_keep, 0.0)
        keep_hi = log_keep.astype(BF16)
        keep_lo = (log_keep - keep_hi.astype(F32)).astype(BF16)
        excl = _dot(ut, keep_hi) + _dot(ut, keep_lo)
        a = jnp.exp(jnp.minimum(z, 0.0) - sp + excl + r_sc[...])
        if mask is not None:
            a = jnp.where(mask, a, 0.0)
        acc_sc[...] = acc_sc[...] + _dot(kvt_ref[kt, d:2 * d, :], a.astype(BF16))
        r_sc[...] = r_sc[...] + jnp.sum(log_keep, axis=0, keepdims=True)

    key = lax.broadcasted_iota(jnp.int32, (tq, 1), 0)
    qry = lax.broadcasted_iota(jnp.int32, (1, H_SB * tq), 1) & (tq - 1)
    tile(qi, key < qry)

    def body(i, carry):
        tile(qi - 1 - i, None)
        return carry

    lax.fori_loop(0, qi, body, 0)
    acc = acc_sc[...]
    o_ref[...] = jnp.concatenate([acc[:, h * tq:(h + 1) * tq] for h in range(H_SB)], axis=0).T


def _sb_prompt(sbq_t, sb_rows, sb_t, ut, *, b, t, tq):
    nq = t // tq
    assert tq & (tq - 1) == 0
    return pl.pallas_call(
        functools.partial(_sb_prompt_kernel, tq=tq),
        grid=(b, nq),
        in_specs=[pl.BlockSpec((None, H_SB * HEAD_DIM, tq), lambda i, j: (i * nq + j, 0, 0)),
                  pl.BlockSpec((t, 2 * HEAD_DIM), lambda i, j: (i, 0)),
                  pl.BlockSpec((nq, 2 * HEAD_DIM, tq), lambda i, j: (i, 0, 0)),
                  _full(ut.shape)],
        out_specs=pl.BlockSpec((tq, H_SB * HEAD_DIM), lambda i, j: (i * nq + j, 0)),
        out_shape=jax.ShapeDtypeStruct((b * t, H_SB * HEAD_DIM), F32),
        scratch_shapes=[pltpu.VMEM((1, H_SB * tq), F32), pltpu.VMEM((HEAD_DIM, H_SB * tq), F32)],
        compiler_params=_cparams(("parallel", "arbitrary")),
        name="sb_prompt",
    )(sbq_t, sb_rows, sb_t, ut)


def _compress_weights(pe_k, w1_k, pe_v, w1_v):
    d = HEAD_DIM
    w1k = w1_k.reshape(CMP_BLOCK, d, -1)
    w1v = w1_v.reshape(CMP_BLOCK, d, -1)
    z = jnp.zeros((d, d), F32)
    mats, pes = [], []
    for i in range(CMP_STRIDE // 2):
        blocks = []
        for r in (2 * i, 2 * i + 1):
            blocks.append(jnp.concatenate([w1k[r], w1k[CMP_STRIDE + r], z, z], axis=1))
            blocks.append(jnp.concatenate([z, z, w1v[r], w1v[CMP_STRIDE + r]], axis=1))
        mats.append(jnp.concatenate(blocks, axis=0))
        row_a = jnp.concatenate([pe_k[2 * i], pe_v[2 * i], pe_k[2 * i + 1], pe_v[2 * i + 1]])
        row_b = jnp.concatenate([pe_k[CMP_STRIDE + 2 * i], pe_v[CMP_STRIDE + 2 * i],
                                 pe_k[CMP_STRIDE + 2 * i + 1], pe_v[CMP_STRIDE + 2 * i + 1]])
        pes.append(jnp.concatenate([row_a[None], row_b[None], jnp.zeros((6, 4 * d), F32)], axis=0))
    return jnp.stack(mats).astype(BF16), jnp.stack(pes).astype(BF16)


def _compress_bias(pe_ref, wc_ref):
    acc = jnp.zeros((8, 4 * HEAD_DIM), F32)
    for i in range(CMP_STRIDE // 2):
        acc = acc + _dot(pe_ref[i], wc_ref[i])
    lane = lax.broadcasted_iota(jnp.int32, (1, 4 * HEAD_DIM), 1)
    is_a = (lane & HEAD_DIM) == 0
    return jnp.where(is_a, acc[0:1, :], acc[1:2, :])


def _compress_finish(ab, w2k_ref, w2v_ref):
    n_sub = ab.shape[0]
    nxt = pltpu.roll(ab, n_sub - 1, axis=0)
    d = HEAD_DIM
    hid_k = _gelu_tanh(ab[:, 0:d] + nxt[:, d:2 * d])
    hid_v = _gelu_tanh(ab[:, 2 * d:3 * d] + nxt[:, 3 * d:4 * d])
    kc = _dot(hid_k.astype(BF16), w2k_ref[...]).astype(BF16)
    vc = _dot(hid_v.astype(BF16), w2v_ref[...]).astype(BF16)
    return kc, vc


def _select_blocks_t(imp, cur, n_blk):
    blk = lax.broadcasted_iota(jnp.int32, (n_blk, 1), 0)
    visible = blk <= cur
    forced = (blk == 0) | (blk == cur) | (blk == cur - 1)
    score = jnp.where(visible, jnp.where(forced, FORCE_SCORE, imp), -1.0)
    rank = jnp.zeros(imp.shape, jnp.int32)
    for b2 in range(n_blk):
        sb2 = score[b2:b2 + 1, :]
        ahead = (sb2 > score) | ((sb2 == score) & (b2 < blk))
        rank = rank + ahead.astype(jnp.int32)
    return ((rank < SEL_TOP_N) & (score >= 0.0)).astype(F32)


def _nsa_prompt_kernel(qt_ref, cmp_ref, slc_ref, slct_ref, win_ref, wint_ref, gatet_ref, wc_ref, pe_ref, w2k_ref,
                       w2v_ref, ovt_ref, eselt_ref, bct_ref, bdt_ref, c31_ref, o_ref,
                       kc_sc, vct_sc, ms_sc, ls_sc, accs_sc, mw_sc, lw_sc, accw_sc, *, tq, n_sub, n_blk):
    qi = pl.program_id(1)
    d = HEAD_DIM

    @pl.when(qi == 0)
    def _():
        ab = jnp.zeros((n_sub, 4 * d), F32)
        for i in range(CMP_STRIDE // 2):
            lhs = jnp.concatenate([cmp_ref[pl.ds(2 * i, n_sub, stride=CMP_STRIDE), :],
                                   cmp_ref[pl.ds(2 * i + 1, n_sub, stride=CMP_STRIDE), :]], axis=1)
            ab = ab + _dot(lhs.astype(BF16), wc_ref[i])
        kc, vc_wide = _compress_finish(ab + _compress_bias(pe_ref, wc_ref), w2k_ref, w2v_ref)
        kc_sc[...] = kc
        vct_sc[...] = vc_wide.astype(F32).T[0:d, :].astype(BF16)

    key, qry = _tile_iotas(tq)
    qpos = qi * tq + qry
    kc, vc_t = kc_sc[...], vct_sc[...]
    c_end = lax.broadcasted_iota(jnp.int32, (n_sub, 1), 0) * CMP_STRIDE + (CMP_BLOCK - 1)
    mask_c = c_end <= qpos

    heads = range(H_NSA)
    ss = [jnp.where(mask_c, _dot(kc, qt_ref[h * d:(h + 1) * d, :]) * ATT_SCALE + bct_ref[h], NEG_BIG) for h in heads]
    es = [jnp.where(mask_c, jnp.exp(s - jnp.max(s, axis=0, keepdims=True)), 0.0) for s in ss]
    ps = [e / jnp.maximum(jnp.sum(e, axis=0, keepdims=True), 1e-30) for e in es]
    o_cmp = [_dot(vc_t, p.astype(BF16)) for p in ps]
    psum = ps[0]
    for p in ps[1:]:
        psum = psum + p
    p_hi = psum.astype(BF16)
    p_lo = (psum - p_hi.astype(F32)).astype(BF16)
    imp = _dot(ovt_ref[...], p_hi) + _dot(ovt_ref[...], p_lo)
    sel = _select_blocks_t(imp, qpos >> 6, n_blk).astype(BF16)

    slc_state = (ms_sc, ls_sc, accs_sc)
    win_state = (mw_sc, lw_sc, accw_sc)
    _softmax_init(*slc_state)
    _softmax_init(*win_state)

    def attend(k_ref, vt_ref, kt, delta, mask, state):
        k0 = pl.multiple_of(kt * tq, tq)
        k = k_ref[pl.ds(k0, tq), 0:d].astype(BF16)
        v_t = vt_ref[kt, d:2 * d, :]
        ss = [_dot(k, qt_ref[h * d:(h + 1) * d, :]) * ATT_SCALE
              + (c31_ref[h] if (delta is None or delta >= 2) else bdt_ref[delta, h]) for h in range(H_NSA)]
        _softmax_update_t(ss, mask, [v_t] * H_NSA, *state)

    def picked(kt):
        return _dot(eselt_ref[kt], sel) > 0.5

    def far_body(kt, carry):
        attend(slc_ref, slct_ref, kt, None, picked(kt), slc_state)
        return carry

    lax.fori_loop(0, jnp.maximum(qi - 1, 0), far_body, 0)

    @pl.when(qi >= 1)
    def _():
        attend(slc_ref, slct_ref, qi - 1, 1, picked(qi - 1), slc_state)

    attend(slc_ref, slct_ref, qi, 0, picked(qi) & (key <= qry), slc_state)

    for delta in reversed(range((WINDOW + tq - 2) // tq + 1)):
        mask = None
        if delta == 0:
            mask = key <= qry
        elif (delta + 1) * tq - 1 >= WINDOW:
            mask = (delta * tq + qry - key) < WINDOW

        @pl.when(qi >= delta)
        def _(delta=delta, mask=mask):
            attend(win_ref, wint_ref, qi - delta, delta, mask, win_state)

    gates = gatet_ref[...]
    mixed = []
    for h in range(H_NSA):
        o_slc = accs_sc[h] / jnp.maximum(ls_sc[h], 1e-30)
        o_win = accw_sc[h] / jnp.maximum(lw_sc[h], 1e-30)
        mixed.append(gates[3 * h:3 * h + 1, :] * o_cmp[h] + gates[3 * h + 1:3 * h + 2, :] * o_slc
                     + gates[3 * h + 2:3 * h + 3, :] * o_win)
    o_ref[...] = jnp.concatenate(mixed, axis=0).T


def _nsa_prompt(nq_t, nsa_rows, slc_t, win_rows, win_t, gates_t, cw, tabs, *, b, t, tq):
    nqb = t // tq
    n_sub = t // CMP_STRIDE
    n_blk = -(-t // SEL_BLOCK)
    wc, pe, w2k, w2v = cw
    ov_t, esel_t, bias_ct, bias_dt, c31 = tabs
    d = HEAD_DIM
    w2v_wide = jnp.pad(w2v, ((0, 0), (0, LANE - d)))
    tile3 = lambda w: pl.BlockSpec((nqb, w, tq), lambda i, j: (i, 0, 0))
    return pl.pallas_call(
        functools.partial(_nsa_prompt_kernel, tq=tq, n_sub=n_sub, n_blk=n_blk),
        grid=(b, nqb),
        in_specs=[pl.BlockSpec((None, H_NSA * d, tq), lambda i, j: (i * nqb + j, 0, 0)),
                  pl.BlockSpec((t, 2 * d), lambda i, j: (i, 0)),
                  pl.BlockSpec((t, 2 * d), lambda i, j: (i, 1)),
                  tile3(2 * d),
                  pl.BlockSpec((t, 2 * d), lambda i, j: (i, 0)),
                  tile3(2 * d),
                  pl.BlockSpec((None, LANE, tq), lambda i, j: (i * nqb + j, 0, 0)),
                  _full(wc.shape), _full(pe.shape), _full(w2k.shape), _full(w2v_wide.shape),
                  _full(ov_t.shape), _full(esel_t.shape),
                  pl.BlockSpec((H_NSA, n_sub, tq), lambda i, j: (0, 0, j)),
                  _full(bias_dt.shape), _full(c31.shape)],
        out_specs=pl.BlockSpec((tq, H_NSA * d), lambda i, j: (i * nqb + j, 0)),
        out_shape=jax.ShapeDtypeStruct((b * t, H_NSA * d), F32),
        scratch_shapes=[pltpu.VMEM((n_sub, d), BF16), pltpu.VMEM((d, n_sub), BF16)]
        + [pltpu.VMEM((H_NSA, 1, tq), F32), pltpu.VMEM((H_NSA, 1, tq), F32), pltpu.VMEM((H_NSA, d, tq), F32)] * 2,
        compiler_params=_cparams(("parallel", "arbitrary")),
        name="nsa_prompt",
    )(nq_t, nsa_rows, nsa_rows, slc_t, win_rows, win_t, gates_t, wc, pe, w2k, w2v_wide, ov_t, esel_t, bias_ct,
      bias_dt, c31)


def _overlap_matrix(n_cmp_pad, n_blk_pad):
    c_start = np.arange(n_cmp_pad)[:, None] * CMP_STRIDE
    b_start = np.arange(n_blk_pad)[None, :] * SEL_BLOCK
    return ((c_start < b_start + SEL_BLOCK) & (c_start + CMP_BLOCK > b_start)).astype(np.float32)


def _nsa_prompt_tables(rel_bias, t, tq):
    n_sub = t // CMP_STRIDE
    n_blk = -(-t // SEL_BLOCK)
    ov_t = jnp.asarray(_overlap_matrix(n_sub, n_blk).T, BF16)
    key_blk = np.arange(t) // SEL_BLOCK
    esel = (key_blk[:, None] == np.arange(n_blk)[None, :]).astype(np.float32)
    esel_t = jnp.asarray(esel.reshape(t // tq, tq, n_blk), BF16)
    qpos = np.arange(t)[None, :]
    dist_c = qpos - (np.arange(n_sub)[:, None] * CMP_STRIDE + CMP_BLOCK - 1)
    bias_ct = jnp.transpose(_bias_lookup(rel_bias, dist_c), (2, 0, 1))
    key = np.arange(tq)[:, None]
    qry = np.arange(tq)[None, :]
    dist_d = np.stack([qry - key, tq + qry - key])
    bias_dt = jnp.transpose(_bias_lookup(rel_bias, dist_d), (0, 3, 1, 2))
    c31 = rel_bias[N_BUCKETS - 1].reshape(H_NSA, 1, 1).astype(F32)
    return ov_t, esel_t, bias_ct, bias_dt, c31


def _sample_paged_kernel(pt_ref, qabs_ref, mnew_ref, sbq_ref, sbnew_ref, nq_ref, wnew_ref, wbuf_ref,
                         cmla_ref, csb_ref, cnsa_ref,
                         wuv_ref, wc_ref, pe_ref, w2k_ref, w2v_ref, u_ref, ov_ref, gq_ref, bc_ref, bw1_ref, bw2_ref,
                         csum_ref, cexp_ref,
                         oa_ref, ob_ref, ocmp_ref, owin_ref, idx_ref,
                         mla_buf, sb_buf, nsa_buf, sem, m_sc, l_sc, acc_sc, r_sc, sbacc_sc, ab_sc,
                         *, layer, n_pg, n_sub, n_blk_pad, past, nq_tok):
    b = pl.program_id(0)
    j = pl.program_id(1)
    n_seq = pl.num_programs(0)
    n_step = pl.num_programs(1)
    d = HEAD_DIM
    c = MLA_KV_RANK

    def page_copies(bb, jj, slot):
        out = []
        for i in range(n_pg):
            pg = pt_ref[bb, (n_step - 1 - jj) * n_pg + i]
            rows = pl.ds(i * LANE, LANE)
            out.append(pltpu.make_async_copy(cmla_ref.at[layer, pg], mla_buf.at[slot, :, rows], sem.at[slot, 0]))
            out.append(pltpu.make_async_copy(csb_ref.at[layer, pg], sb_buf.at[slot, rows], sem.at[slot, 1]))
            out.append(pltpu.make_async_copy(cnsa_ref.at[layer, pg, :, pl.ds(0, 2 * d)], nsa_buf.at[slot, rows],
                                             sem.at[slot, 2]))
        return out

    step = b * n_step + j
    slot = step & 1

    @pl.when(step == 0)
    def _():
        for cp in page_copies(b, j, slot):
            cp.start()

    @pl.when(step + 1 < n_seq * n_step)
    def _():
        wrap = j + 1 == n_step
        for cp in page_copies(jnp.where(wrap, b + 1, b), jnp.where(wrap, 0, j + 1), 1 - slot):
            cp.start()

    for cp in page_copies(b, j, slot):
        cp.wait()
    q_mla = qabs_ref[...]
    q_sb = sbq_ref[...]
    u_tri = u_ref[...]
    row_q = lax.broadcasted_iota(jnp.int32, (ROWS_Q, 1), 0) >> 3
    row_q_sb = lax.broadcasted_iota(jnp.int32, (nq_tok * H_SB, 1), 0) >> 2
    new_col = lax.broadcasted_iota(jnp.int32, (1, NEW_PAD), 1)

    @pl.when(j == 0)
    def _():
        mnew = mnew_ref[...].astype(BF16)
        s = _dot_nt(q_mla, mnew) * MLA_SCALE
        p, _, m, l = _softmax_step(s, new_col <= row_q, jnp.full((ROWS_Q, 1), NEG_BIG, F32),
                                   jnp.zeros((ROWS_Q, 1), F32))
        m_sc[...] = m
        l_sc[...] = l
        acc_sc[...] = _dot(p.astype(BF16), mnew[:, :c])
        sbnew = sbnew_ref[...].astype(BF16)
        z = _dot_nt(q_sb, sbnew[:, :d]) * ATT_SCALE
        o, r_sum = _sb_tile(z, new_col < row_q_sb, jnp.zeros((nq_tok * H_SB, 1), F32), sbnew[:, d:],
                            u_tri[:NEW_PAD, :NEW_PAD])
        sbacc_sc[...] = o
        r_sc[...] = r_sum

    n_chunk = n_pg * LANE // SB_CHUNK
    n_row = nq_tok * H_SB
    keys_t = mla_buf[slot].astype(BF16)
    kv = sb_buf[slot].astype(BF16)
    s = _dot(q_mla, keys_t) * MLA_SCALE
    z = _dot_nt(q_sb, kv[:, :d]) * ATT_SCALE
    ab = jnp.zeros((n_pg * 8, 4 * d), F32)
    for i2 in range(CMP_STRIDE // 2):
        lhs = jnp.concatenate([nsa_buf[slot, pl.ds(2 * i2, n_pg * 8, stride=CMP_STRIDE), :],
                               nsa_buf[slot, pl.ds(2 * i2 + 1, n_pg * 8, stride=CMP_STRIDE), :]], axis=1)
        ab = ab + _dot(lhs.astype(BF16), wc_ref[i2])
    sub0 = pl.multiple_of((n_step - 1 - j) * (n_pg * 8), n_pg * 8)
    ab_sc[pl.ds(sub0, n_pg * 8), :] = ab

    m_old = m_sc[...]
    m_new = jnp.maximum(m_old, jnp.max(s, axis=1, keepdims=True))
    sp = jnp.log(1.0 + jnp.exp(-jnp.abs(z)))
    log_keep = -jnp.maximum(z, 0.0) - sp
    keep_hi = log_keep.astype(BF16)
    keep_lo = (log_keep - keep_hi.astype(F32)).astype(BF16)
    keep2 = jnp.concatenate([keep_hi, keep_lo], axis=0)
    p = jnp.exp(s - m_new)
    alpha = jnp.exp(m_old - m_new)

    e2 = [_dot(keep2[:, ch * SB_CHUNK:(ch + 1) * SB_CHUNK], u_tri) for ch in range(n_chunk)]
    c2 = _dot(keep2, csum_ref[...])
    pv = _dot_nt(p.astype(BF16), keys_t[:c, :])
    excl = jnp.concatenate([e[:n_row] + e[n_row:] for e in e2], axis=1)
    csum = c2[:n_row] + c2[n_row:]
    right = _dot_hilo(csum, cexp_ref[...])
    m_sc[...] = m_new
    l_sc[...] = alpha * l_sc[...] + jnp.sum(p, axis=1, keepdims=True)
    acc_sc[...] = alpha * acc_sc[...] + pv
    a = jnp.exp(jnp.minimum(z, 0.0) - sp + excl + right + r_sc[...])
    sbacc_sc[...] = sbacc_sc[...] + _dot(a.astype(BF16), kv[:, d:])
    r_sc[...] = r_sc[...] + csum[:, n_chunk:n_chunk + 1]

    @pl.when(j == n_step - 1)
    def _():
        o_lat = (acc_sc[...] / jnp.maximum(l_sc[...], 1e-30)).astype(BF16)
        res = _dot(o_lat, wuv_ref[...])
        head = lax.broadcasted_iota(jnp.int32, (ROWS_Q, 1), 0) & 7
        oa = jnp.zeros((ROWS_Q, MLA_V), F32)
        for h in range(H_MLA):
            oa = oa + jnp.where(head == h, res[:, h * MLA_V:(h + 1) * MLA_V], 0.0)
        oa_ref[...] = oa
        ob_ref[...] = sbacc_sc[...]

        kc, vc = _compress_finish(ab_sc[...] + _compress_bias(pe_ref, wc_ref), w2k_ref, w2v_ref)
        q_n = nq_ref[...]
        c_idx = lax.broadcasted_iota(jnp.int32, (1, n_sub), 1)
        mask_c = (c_idx * CMP_STRIDE + (CMP_BLOCK - 1)) <= (past + row_q)
        s_c = jnp.where(mask_c, _dot_nt(q_n, kc) * ATT_SCALE + bc_ref[...], NEG_BIG)
        e = jnp.where(mask_c, jnp.exp(s_c - jnp.max(s_c, axis=1, keepdims=True)), 0.0)
        p_c = e / jnp.maximum(jnp.sum(e, axis=1, keepdims=True), 1e-30)
        ocmp_ref[...] = _dot(p_c.astype(BF16), vc)
        p_hi = p_c.astype(BF16)
        p_lo = (p_c - p_hi.astype(F32)).astype(BF16)
        psum = _dot(gq_ref[...], p_hi) + _dot(gq_ref[...], p_lo)
        imp = _dot_hilo(psum, ov_ref[...])

        blk = lax.broadcasted_iota(jnp.int32, (8, n_blk_pad), 1)
        blk_f = blk.astype(F32)
        cur = (past + lax.broadcasted_iota(jnp.int32, (8, 1), 0)) >> 6
        forced = (blk == 0) | (blk == cur) | (blk == cur - 1)
        score = jnp.where(blk <= cur, jnp.where(forced, FORCE_SCORE, imp), -1.0)
        lane = lax.broadcasted_iota(jnp.int32, (8, LANE), 1)
        picked = jnp.full((8, LANE), -1.0, F32)
        for t in range(SEL_TOP_N):
            best = jnp.max(score, axis=1, keepdims=True)
            arg = jnp.min(jnp.where(score == best, blk_f, float(n_blk_pad)), axis=1, keepdims=True)
            picked = jnp.where(lane == t, jnp.where(best >= 0.0, arg, -1.0), picked)
            score = jnp.where(blk_f == arg, -3e38, score)
        idx_ref[...] = picked.astype(jnp.int32)

        wbuf = wbuf_ref[...].astype(BF16)
        wnew = wnew_ref[...].astype(BF16)
        n_buf = wbuf.shape[0]
        buf_col = lax.broadcasted_iota(jnp.int32, (1, n_buf), 1)
        dist1 = n_buf + row_q - buf_col
        mask1 = (dist1 >= 0) & (dist1 < WINDOW)
        mask2 = new_col <= row_q
        s1 = jnp.where(mask1, _dot_nt(q_n, wbuf[:, :d]) * ATT_SCALE + bw1_ref[...], NEG_BIG)
        s2 = jnp.where(mask2, _dot_nt(q_n, wnew[:, :d]) * ATT_SCALE + bw2_ref[...], NEG_BIG)
        m_w = jnp.maximum(jnp.max(s1, axis=1, keepdims=True), jnp.max(s2, axis=1, keepdims=True))
        e1 = jnp.where(mask1, jnp.exp(s1 - m_w), 0.0)
        e2 = jnp.where(mask2, jnp.exp(s2 - m_w), 0.0)
        l_w = jnp.sum(e1, axis=1, keepdims=True) + jnp.sum(e2, axis=1, keepdims=True)
        o_w = _dot(e1.astype(BF16), wbuf[:, d:]) + _dot(e2.astype(BF16), wnew[:, d:])
        owin_ref[...] = o_w / jnp.maximum(l_w, 1e-30)


def _sample_paged(layer, page_table, qabs, mla_new, sbq, sb_new, nq, win_new, win_buf,
                  cache_mla, cache_sb, cache_nsa, consts, *, past):
    nb, n_pages = page_table.shape
    n_pg = min(PAGES_PER_STEP, n_pages)
    n_step = n_pages // n_pg
    n_sub = past // CMP_STRIDE
    nq_tok = sbq.shape[1] // H_SB
    wuv, wc, pe, w2k, w2v, u_tri, ov, gq, bias_c, bias_w1, bias_w2 = consts
    n_blk_pad = ov.shape[1]
    d = HEAD_DIM

    def seq(a):
        return pl.BlockSpec((None,) + a.shape[1:], lambda b, j, pt: (b,) + (0,) * (a.ndim - 1))

    cst = lambda a: pl.BlockSpec(a.shape, lambda b, j, pt: (0,) * a.ndim)
    seq_in = [qabs, mla_new, sbq, sb_new, nq, win_new, win_buf]
    n_chunk = n_pg * LANE // SB_CHUNK
    key_chunk = np.arange(n_pg * LANE) // SB_CHUNK
    lane_c = np.arange(LANE)
    csum_m = ((key_chunk[:, None] > lane_c[None, :]) & (lane_c[None, :] < n_chunk)) | (lane_c[None, :] == n_chunk)
    cexp_m = (lane_c[:, None] == key_chunk[None, :])
    const_in = [wuv, wc, pe, w2k, w2v, u_tri, ov, gq, bias_c, bias_w1, bias_w2,
                jnp.asarray(csum_m, BF16), jnp.asarray(cexp_m, BF16)]
    in_specs = ([seq(a) for a in seq_in] + [pl.BlockSpec(memory_space=pl.ANY)] * 3 + [cst(a) for a in const_in])
    out_shape = [jax.ShapeDtypeStruct((nb, ROWS_Q, MLA_V), F32),
                 jax.ShapeDtypeStruct((nb, nq_tok * H_SB, d), F32),
                 jax.ShapeDtypeStruct((nb, ROWS_Q, d), F32),
                 jax.ShapeDtypeStruct((nb, ROWS_Q, d), F32),
                 jax.ShapeDtypeStruct((nb, 8, LANE), jnp.int32)]
    out_specs = [pl.BlockSpec((None,) + s.shape[1:], lambda b, j, pt: (b, 0, 0)) for s in out_shape]
    kernel = functools.partial(_sample_paged_kernel, layer=layer, n_pg=n_pg, n_sub=n_sub, n_blk_pad=n_blk_pad,
                               past=past, nq_tok=nq_tok)
    return pl.pallas_call(
        kernel,
        grid_spec=pltpu.PrefetchScalarGridSpec(
            num_scalar_prefetch=1,
            grid=(nb, n_step),
            in_specs=in_specs,
            out_specs=out_specs,
            scratch_shapes=[pltpu.VMEM((2, MLA_KV_RANK + MLA_ROPE, n_pg * LANE), F32),
                            pltpu.VMEM((2, n_pg * LANE, 2 * d), F32),
                            pltpu.VMEM((2, n_pg * LANE, 2 * d), F32),
                            pltpu.SemaphoreType.DMA((2, 3)),
                            pltpu.VMEM((ROWS_Q, 1), F32), pltpu.VMEM((ROWS_Q, 1), F32),
                            pltpu.VMEM((ROWS_Q, MLA_KV_RANK), F32),
                            pltpu.VMEM((nq_tok * H_SB, 1), F32), pltpu.VMEM((nq_tok * H_SB, d), F32),
                            pltpu.VMEM((n_sub, 4 * d), F32)]),
        out_shape=out_shape,
        compiler_params=_cparams(("arbitrary", "arbitrary")),
        name="sample_paged",
    )(page_table, *seq_in, jnp.swapaxes(cache_mla, 2, 3), cache_sb, cache_nsa, *const_in)


def _sample_slc_kernel(idx_ref, pt_ref, nq_ref, nnew_ref, ocmp_ref, owin_ref, gate_ref, tsp_ref, c31_ref, cache_ref,
                       o_ref, kv_buf, sem, *, layer, past, nq_tok):
    b = pl.program_id(0)
    d = HEAD_DIM
    n_past_blk = past // SEL_BLOCK
    blk_per_page = LANE // SEL_BLOCK

    def copies(q):
        out = []
        for k in range(SEL_TOP_N):
            blk = jnp.clip(idx_ref[b, q, k], 0, n_past_blk - 1)
            page = pt_ref[b, blk // blk_per_page]
            r0 = pl.multiple_of((blk % blk_per_page) * SEL_BLOCK, SEL_BLOCK)
            out.append(pltpu.make_async_copy(
                cache_ref.at[layer, page, pl.ds(r0, SEL_BLOCK), pl.ds(2 * d, 2 * d)], kv_buf.at[q, k], sem.at[q]))
        return out

    for q in range(nq_tok):
        for cp in copies(q):
            cp.start()

    lane = lax.broadcasted_iota(jnp.int32, (1, SEL_TOP_N * SEL_BLOCK), 1)
    slot = lane >> 6
    new_col = lax.broadcasted_iota(jnp.int32, (1, NEW_PAD), 1)
    nnew = nnew_ref[...].astype(BF16)
    c31 = c31_ref[...]
    for q in range(nq_tok):
        for cp in copies(q):
            cp.wait()
        idx_vec = jnp.full(lane.shape, -1, jnp.int32)
        for k in range(SEL_TOP_N):
            idx_vec = jnp.where(slot == k, idx_ref[b, q, k], idx_vec)
        has_new = jnp.max((idx_vec == n_past_blk).astype(jnp.int32), axis=1, keepdims=True) > 0
        kpos = idx_vec * SEL_BLOCK + (lane & (SEL_BLOCK - 1))
        mask = (idx_vec >= 0) & (idx_vec < n_past_blk) & (kpos <= past + q)
        rows = slice(8 * q, 8 * q + 8)
        bias = jnp.broadcast_to(c31[rows, 0:1], (8, SEL_TOP_N * SEL_BLOCK))
        for t in (1, 2):
            bias = jnp.where(idx_vec == n_past_blk - t, tsp_ref[t, rows, :], bias)
        kv = kv_buf[q].reshape(SEL_TOP_N * SEL_BLOCK, 2 * d).astype(BF16)
        qn = nq_ref[rows, :].astype(BF16)
        s1 = jnp.where(mask, _dot_nt(qn, kv[:, :d]) * ATT_SCALE + bias, NEG_BIG)
        mask2 = (new_col <= q) & (new_col < nq_tok) & has_new
        s2 = jnp.where(mask2, _dot_nt(qn, nnew[:, 2 * d:3 * d]) * ATT_SCALE + tsp_ref[0, rows, 0:NEW_PAD], NEG_BIG)
        m = jnp.maximum(jnp.max(s1, axis=1, keepdims=True), jnp.max(s2, axis=1, keepdims=True))
        e1 = jnp.where(mask, jnp.exp(s1 - m), 0.0)
        e2 = jnp.where(mask2, jnp.exp(s2 - m), 0.0)
        l = jnp.sum(e1, axis=1, keepdims=True) + jnp.sum(e2, axis=1, keepdims=True)
        o_slc = (_dot(e1.astype(BF16), kv[:, d:]) + _dot(e2.astype(BF16), nnew[:, 3 * d:])) / jnp.maximum(l, 1e-30)
        g = gate_ref[rows, :]
        o_ref[rows, :] = g[:, 0:1] * ocmp_ref[rows, :] + g[:, 1:2] * o_slc + g[:, 2:3] * owin_ref[rows, :]


def _sample_slc(layer, idx, page_table, nq32, nsa_new, o_cmp, o_win, gates32, tsp, c31, cache_nsa, *, past):
    nb = page_table.shape[0]
    nq_tok = idx.shape[1]
    d = HEAD_DIM
    seq = lambda a: pl.BlockSpec((None,) + a.shape[1:], lambda b, i, p: (b,) + (0,) * (a.ndim - 1))
    cst = lambda a: pl.BlockSpec(a.shape, lambda b, i, p: (0,) * a.ndim)
    return pl.pallas_call(
        functools.partial(_sample_slc_kernel, layer=layer, past=past, nq_tok=nq_tok),
        grid_spec=pltpu.PrefetchScalarGridSpec(
            num_scalar_prefetch=2,
            grid=(nb,),
            in_specs=[seq(nq32), seq(nsa_new), seq(o_cmp), seq(o_win), seq(gates32), cst(tsp), cst(c31),
                      pl.BlockSpec(memory_space=pl.ANY)],
            out_specs=pl.BlockSpec((None, ROWS_Q, d), lambda b, i, p: (b, 0, 0)),
            scratch_shapes=[pltpu.VMEM((nq_tok, SEL_TOP_N, SEL_BLOCK, 2 * d), F32),
                            pltpu.SemaphoreType.DMA((nq_tok,))]),
        out_shape=jax.ShapeDtypeStruct((nb, ROWS_Q, d), F32),
        compiler_params=_cparams(("arbitrary",)),
        name="sample_slc",
    )(idx, page_table, nq32, nsa_new, o_cmp, o_win, gates32, tsp, c31, cache_nsa)


def _sample_tables(rel_bias, past, nq_tok, n_buf):
    n_sub = past // CMP_STRIDE
    n_blk = -(-(past + nq_tok) // SEL_BLOCK)
    n_blk_pad = -(-n_blk // LANE) * LANE
    ov = jnp.asarray(_overlap_matrix(n_sub, n_blk_pad), BF16)
    gq = np.zeros((8, ROWS_Q), np.float32)
    for q in range(nq_tok):
        gq[q, 8 * q:8 * q + H_NSA] = 1.0
    qpos = past + np.arange(nq_tok)[:, None]
    bias_c = _bias_rows(rel_bias, qpos - (np.arange(n_sub)[None, :] * CMP_STRIDE + CMP_BLOCK - 1), 8)
    bias_w1 = _bias_rows(rel_bias, qpos - (past - n_buf + np.arange(n_buf)[None, :]), 8)
    bias_w2 = _bias_rows(rel_bias, qpos - (past + np.arange(NEW_PAD)[None, :]), 8)
    s = np.arange(SEL_BLOCK)[None, :]
    tsp = jnp.stack([jnp.tile(_bias_rows(rel_bias, qpos - (past - t * SEL_BLOCK + s), 8), (1, SEL_TOP_N))
                     for t in range(3)])
    c31 = jnp.pad(jnp.broadcast_to(rel_bias[N_BUCKETS - 1][None, :, None], (nq_tok, H_NSA, LANE)),
                  ((0, 0), (0, 8 - H_NSA), (0, 0))).reshape(ROWS_Q, LANE).astype(F32)
    return ov, jnp.asarray(gq, BF16), bias_c, bias_w1, bias_w2, tsp, c31


def _rows_q8(a, nb, nq_tok, h):
    w = a.shape[1] // h
    a = a.reshape(nb, nq_tok, h, w)
    a = jnp.pad(a, ((0, 0), (0, 0), (0, 8 - h), (0, 0)))
    return a.reshape(nb, nq_tok * 8, w)


def _pad_new(a, nb, nq_tok):
    a = a.reshape(nb, nq_tok, a.shape[-1])
    return jnp.pad(a, ((0, 0), (0, NEW_PAD - nq_tok), (0, 0)))


def kernel(x_prompt, x_sample, cache_mla, cache_sb, cache_nsa, state_win, page_table, p_prompt, p_sample, g_attn, w_in, g_cq, g_ckv, w_uq, w_uk, w_uv, cmp_pe_k, cmp_w1_k, cmp_w2_k, cmp_pe_v, cmp_w1_v, cmp_w2_v, rel_bias, g_grp_mla, g_grp_sb, g_grp_nsa, w_out, g_ffn, w_gate, w_up, w_down, g_ple, w_ple_gate, w_ple, g_final):
    bsz, t, dm = x_prompt.shape
    nb, nq_tok, _ = x_sample.shape
    depth = w_in.shape[0]
    n_pages = page_table.shape[1]
    past = n_pages * cache_mla.shape[2]
    n_buf = state_win.shape[2]
    assert cache_mla.shape[2] == LANE and nq_tok * 8 == ROWS_Q and SEL_BLOCK == 64
    assert past % (PAGES_PER_STEP * LANE) == 0 or n_pages < PAGES_PER_STEP
    tq = min(256, t)
    tm_p = min(512, bsz * t)
    tm_s = min(256, nb * nq_tok)
    win_keep = min(WINDOW, t)
    d = HEAD_DIM

    tab_p = _rope_tables(jnp.arange(t, dtype=jnp.int32))
    tab_s = _rope_tables(jnp.tile(past + jnp.arange(nq_tok, dtype=jnp.int32), tm_s // nq_tok))
    u_tri = jnp.asarray(np.tril(np.ones((SB_CHUNK, SB_CHUNK), np.float32), -1), BF16)
    ptabs = _nsa_prompt_tables(rel_bias, t, tq)
    ov_s, gq, bias_c, bias_w1, bias_w2, tsp, c31_s = _sample_tables(rel_bias, past, nq_tok, n_buf)

    xp = x_prompt.reshape(bsz * t, dm)
    xs = x_sample.reshape(nb * nq_tok, dm)
    outs = [[] for _ in range(8)]
    for i in range(depth):
        final = i == depth - 1
        pw = _proj_weights(w_in[i], w_uq[i], w_uk[i], w_uv[i])
        fw = _finish_weights(g_grp_mla[i], g_grp_sb[i], g_grp_nsa[i], w_out[i], g_ffn[i], w_gate[i], w_up[i],
                             w_down[i], g_ple[i], w_ple_gate[i], w_ple[i], g_final)
        wc, pe = _compress_weights(cmp_pe_k[i], cmp_w1_k[i], cmp_pe_v[i], cmp_w1_v[i])
        w2k, w2v = cmp_w2_k[i].astype(BF16), cmp_w2_v[i].astype(BF16)

        (mla_r, sb_r, nsa_r, win_r, gates_t, sbq_t, nq_t, qa_t, ka, va_t, sb_t, slc_t, win_t) = _proj(
            xp, tab_p, pw, g_attn[i], g_cq[i], g_ckv[i], prompt=True, tm=tm_p, table_period=t, tq=tq)
        o_a = _mla_prompt(qa_t, ka, va_t, b=bsz, t=t, tq=tq)
        o_b = _sb_prompt(sbq_t, sb_r, sb_t, u_tri[:tq, :tq].T, b=bsz, t=t, tq=tq)
        o_c = _nsa_prompt(nq_t, nsa_r, slc_t, win_r, win_t, gates_t, (wc, pe, w2k, w2v), ptabs, b=bsz, t=t, tq=tq)
        xp = _finish(xp, o_a, o_b, o_c, p_prompt[i].reshape(bsz * t, -1), fw, final=final, tm=tm_p)
        outs[0].append(mla_r.reshape(bsz, t, -1))
        outs[2].append(sb_r.reshape(bsz, t, -1))
        outs[4].append(nsa_r.reshape(bsz, t, -1))
        outs[6].append(win_r.reshape(bsz, t, -1)[:, t - win_keep:])

        (mla_n, sb_n, nsa_n, win_n, gates_s, sbq_s, nq_s, qa_s, qlat_s) = _proj(
            xs, tab_s, pw, g_attn[i], g_cq[i], g_ckv[i], prompt=False, tm=tm_s, table_period=tm_s)
        q_pe = qa_s.reshape(nb * nq_tok, H_MLA, SLOT)[:, :, MLA_NOPE:MLA_NOPE + MLA_ROPE]
        qabs = jnp.concatenate([qlat_s.reshape(nb * nq_tok, H_MLA, MLA_KV_RANK), q_pe], axis=2)
        qabs = _rows_q8(qabs.reshape(nb * nq_tok, -1), nb, nq_tok, H_MLA)
        sbq3 = sbq_s.reshape(nb, nq_tok * H_SB, d)
        nq32 = _rows_q8(nq_s, nb, nq_tok, H_NSA)
        consts = (w_uv[i].reshape(MLA_KV_RANK, -1).astype(BF16), wc, pe, w2k, w2v, u_tri, ov_s, gq, bias_c,
                  bias_w1, bias_w2)
        o_a8, o_b4, o_cmp, o_win, idx = _sample_paged(
            i, page_table, qabs, _pad_new(mla_n, nb, nq_tok), sbq3, _pad_new(sb_n, nb, nq_tok), nq32,
            _pad_new(win_n, nb, nq_tok), state_win[i], cache_mla, cache_sb, cache_nsa, consts, past=past)
        gates32 = _rows_q8(gates_s[:, :3 * H_NSA], nb, nq_tok, H_NSA)
        gates32 = jnp.pad(gates32, ((0, 0), (0, 0), (0, LANE - 3)))
        o_c8 = _sample_slc(i, idx[:, :nq_tok, :SEL_TOP_N], page_table, nq32.astype(F32),
                           _pad_new(nsa_n, nb, nq_tok), o_cmp, o_win, gates32, tsp, c31_s, cache_nsa, past=past)
        unrow = lambda a, h: a.reshape(nb, nq_tok, 8, -1)[:, :, :h].reshape(nb * nq_tok, -1)
        xs = _finish(xs, unrow(o_a8, H_MLA), o_b4.reshape(nb * nq_tok, -1), unrow(o_c8, H_NSA),
                     p_sample[i].reshape(nb * nq_tok, -1), fw, final=final, tm=tm_s)
        outs[1].append(mla_n.reshape(nb, nq_tok, -1))
        outs[3].append(sb_n.reshape(nb, nq_tok, -1))
        outs[5].append(nsa_n.reshape(nb, nq_tok, -1))
        win_all = jnp.concatenate([state_win[i], win_n.reshape(nb, nq_tok, -1)], axis=1)
        outs[7].append(win_all[:, nq_tok:])

    y_prompt = xp.reshape(bsz, t, dm)
    y_sample = xs.reshape(nb, nq_tok, dm)
    st = [jnp.stack(o) for o in outs]
    return (y_prompt, y_sample, st[0], st[1], st[2], st[3], st[4], st[5], st[6], st[7])
```

```python
import functools
import math

import numpy as np
import jax
import jax.numpy as jnp
from jax import lax
from jax.experimental import pallas as pl
from jax.experimental.pallas import tpu as pltpu

F32 = jnp.float32
BF16 = jnp.bfloat16

HEAD_DIM = 64
H_MLA = 6
H_SB = 4
H_NSA = 6
MLA_Q_RANK = 256
MLA_KV_RANK = 256
MLA_NOPE = 64
MLA_ROPE = 32
MLA_V = 64
MLA_SCALE = (MLA_NOPE + MLA_ROPE) ** -0.5
ATT_SCALE = HEAD_DIM ** -0.5
ROPE_BASE = 10000.0
CMP_BLOCK = 32
CMP_STRIDE = 16
SEL_BLOCK = 64
SEL_TOP_N = 16
FORCE_SCORE = 1e4
WINDOW = 512
N_BUCKETS = 32
MAX_DISTANCE = 128
EPS = 1e-6
NEG_BIG = -1e30
IN_SPLITS = [MLA_Q_RANK, MLA_KV_RANK, MLA_ROPE, H_SB * HEAD_DIM, 2 * HEAD_DIM,
             H_NSA * HEAD_DIM, 4 * HEAD_DIM, 2 * HEAD_DIM, 3 * H_NSA]

LANE = 128
SLOT = 128
ROPE_HALF = MLA_ROPE // 2
VMEM_LIMIT = 56 * 1024 * 1024
PAGES_PER_STEP = 32
SB_CHUNK = 256
ROWS_Q = 32
NEW_PAD = 16


def _cparams(sem):
    return pltpu.CompilerParams(dimension_semantics=sem, vmem_limit_bytes=VMEM_LIMIT)


def _rms(x, g):
    return x * lax.rsqrt(jnp.mean(x * x, axis=-1, keepdims=True) + EPS) * g


def _dot(a, b):
    return jnp.dot(a, b, preferred_element_type=F32)


def _dot_nt(a, b):
    return lax.dot_general(a, b, (((1,), (1,)), ((), ())), preferred_element_type=F32)


def _dot_hilo(x, w):
    hi = x.astype(BF16)
    lo = (x - hi.astype(F32)).astype(BF16)
    return _dot(hi, w) + _dot(lo, w)


def _gelu_tanh(x):
    return 0.5 * x * (1.0 + jnp.tanh(math.sqrt(2.0 / math.pi) * (x + 0.044715 * (x * x * x))))


def _full(shape):
    n = len(shape)
    return pl.BlockSpec(shape, lambda *_: (0,) * n)


def _t5_bucket_np(dist):
    n = np.maximum(dist, 0)
    max_exact = N_BUCKETS // 2
    nf = np.maximum(n, 1).astype(np.float32)
    large = max_exact + (np.log(nf / np.float32(max_exact)) / np.float32(math.log(MAX_DISTANCE / max_exact))
                         * np.float32(N_BUCKETS - max_exact)).astype(np.int32)
    large = np.minimum(large, N_BUCKETS - 1)
    return np.where(n < max_exact, n, large).astype(np.int32)


def _bias_lookup(rel_bias, dist):
    bucket = jnp.asarray(_t5_bucket_np(dist))
    one_hot = (bucket[..., None] == jnp.arange(N_BUCKETS, dtype=jnp.int32)).astype(F32)
    return jnp.dot(one_hot, rel_bias.astype(F32), precision=lax.Precision.HIGHEST)


def _bias_rows(rel_bias, dist, rows_per_q):
    nq, k = dist.shape
    b = _bias_lookup(rel_bias, dist)
    b = jnp.transpose(b, (0, 2, 1))
    b = jnp.pad(b, ((0, 0), (0, rows_per_q - H_NSA), (0, 0)))
    return b.reshape(nq * rows_per_q, k).astype(F32)


def _rope_tables(pos):
    freqs = ROPE_BASE ** (-jnp.arange(ROPE_HALF, dtype=F32) / ROPE_HALF)
    ang = pos.astype(F32)[:, None] * freqs[None, :]
    cos, sin = jnp.cos(ang), jnp.sin(ang)
    n = pos.shape[0]
    one = jnp.ones((n, MLA_NOPE), F32)
    zq = jnp.zeros((n, SLOT - MLA_NOPE - MLA_ROPE), F32)
    cq = jnp.concatenate([one, cos, cos, zq], axis=1)
    sq = jnp.concatenate([0 * one, sin, sin, zq], axis=1)
    zk = jnp.zeros((n, LANE - MLA_ROPE), F32)
    ck = jnp.concatenate([cos, cos, zk], axis=1)
    sk = jnp.concatenate([sin, sin, zk], axis=1)
    return cq, sq, ck, sk


U_CQ, U_CKV, U_SBQ, U_SBR, U_NQ, U_NR, U_WIN, U_GATE, U_KPE, U_KPES, U_END = (
    0, 256, 512, 768, 896, 1280, 1536, 1664, 1792, 1920, 2048)


def _proj_weights(w_in, w_uq, w_uk, w_uv):
    cuts = [int(c) for c in np.cumsum(IN_SPLITS[:-1])]
    cq, ckv, kpe, sbq, sbr, nq, nr, wr, gt = jnp.split(w_in, cuts, axis=1)
    pad = lambda w, n: jnp.pad(w, ((0, 0), (0, n - w.shape[1])))
    kpe_sw = jnp.concatenate([-kpe[:, ROPE_HALF:], kpe[:, :ROPE_HALF]], axis=1)
    w_all = jnp.concatenate([cq, ckv, sbq, sbr, nq, nr, wr, pad(gt, LANE), pad(kpe, LANE), pad(kpe_sw, LANE)],
                            axis=1).astype(BF16)
    r = w_uq.shape[0]
    zpad = jnp.zeros((r, H_MLA, SLOT - MLA_NOPE - MLA_ROPE), F32)
    wq = jnp.concatenate([w_uq, zpad], axis=2).reshape(r, H_MLA * SLOT).astype(BF16)
    pe1 = w_uq[:, :, MLA_NOPE:MLA_NOPE + ROPE_HALF]
    pe2 = w_uq[:, :, MLA_NOPE + ROPE_HALF:]
    wqs = jnp.concatenate([jnp.zeros((r, H_MLA, MLA_NOPE), F32), -pe2, pe1, zpad], axis=2)
    wqs = wqs.reshape(r, H_MLA * SLOT).astype(BF16)
    c = w_uk.shape[0]
    wk = jnp.concatenate([w_uk, jnp.zeros((c, H_MLA, SLOT - MLA_NOPE), F32)], axis=2)
    wk = wk.reshape(c, H_MLA * SLOT).astype(BF16)
    wv = w_uv.reshape(c, H_MLA * MLA_V).astype(BF16)
    wukt = jnp.transpose(w_uk, (1, 2, 0))
    wukt = jnp.pad(wukt, ((0, 0), (0, SLOT - MLA_NOPE), (0, 0)))
    eye = jnp.eye(H_MLA, dtype=F32)
    wukbd = (eye[:, None, :, None] * wukt[:, :, None, :]).reshape(H_MLA * SLOT, H_MLA * c).astype(BF16)
    return w_all, wq, wqs, wk, wv, wukbd


def _proj_kernel(*refs, prompt, tq):
    (x_ref, gat_ref, wall_ref, gcq_ref, gckv_ref, wq_ref, wqs_ref, cq_ref, sq_ref, ck_ref, sk_ref) = refs[:11]
    if prompt:
        wk_ref, wv_ref = refs[11:13]
        mla_o, sb_o, nsa_o, win_o, gate_o, sbq_o, nq_o, qa_o, ka_o, va_o, sbt_o, slct_o, wint_o = refs[13:]
    else:
        (wukbd_ref,) = refs[11:12]
        mla_o, sb_o, nsa_o, win_o, gate_o, sbq_o, nq_o, qa_o, qlat_o = refs[12:]

    def put(o_ref, val):
        if not prompt:
            o_ref[...] = val.astype(o_ref.dtype)
            return
        for c in range(o_ref.shape[0]):
            o_ref[c] = val[c * tq:(c + 1) * tq, :].T.astype(o_ref.dtype)

    xn = _rms(x_ref[...], gat_ref[...]).astype(BF16)
    u = _dot(xn, wall_ref[...])
    put(sbq_o, u[:, U_SBQ:U_SBR])
    sb_o[...] = u[:, U_SBR:U_NQ]
    put(nq_o, u[:, U_NQ:U_NR])
    nsa_o[...] = u[:, U_NR:U_WIN]
    win_o[...] = u[:, U_WIN:U_GATE]
    put(gate_o, jax.nn.sigmoid(u[:, U_GATE:U_KPE]))
    if prompt:
        put(sbt_o, u[:, U_SBR:U_NQ])
        put(slct_o, u[:, U_NR + 2 * HEAD_DIM:U_WIN])
        put(wint_o, u[:, U_WIN:U_GATE])
    kr = u[:, U_KPE:U_KPES] * ck_ref[...] + u[:, U_KPES:U_END] * sk_ref[...]
    ckvn = _rms(u[:, U_CKV:U_SBQ], gckv_ref[...])
    mla_o[:, 0:MLA_KV_RANK] = ckvn
    mla_o[:, MLA_KV_RANK:MLA_KV_RANK + MLA_ROPE] = kr[:, 0:MLA_ROPE]
    cqn = _rms(u[:, U_CQ:U_CKV], gcq_ref[...]).astype(BF16)
    qf = _dot(cqn, wq_ref[...])
    qs = _dot(cqn, wqs_ref[...])
    cq_t, sq_t = cq_ref[...], sq_ref[...]
    qrot = jnp.concatenate([qf[:, h * SLOT:(h + 1) * SLOT] * cq_t + qs[:, h * SLOT:(h + 1) * SLOT] * sq_t
                            for h in range(H_MLA)], axis=1)
    put(qa_o, qrot)
    if prompt:
        ckb = ckvn.astype(BF16)
        kn = _dot(ckb, wk_ref[...])
        krr = pltpu.roll(kr, MLA_NOPE, axis=1)
        for h in range(H_MLA):
            sl = slice(h * SLOT, (h + 1) * SLOT)
            ka_o[:, sl] = (kn[:, sl] + krr).astype(BF16)
        put(va_o, _dot(ckb, wv_ref[...]))
    else:
        qlat_o[...] = _dot(qrot.astype(BF16), wukbd_ref[...]).astype(BF16)


def _proj(x, tables, pw, g_attn, g_cq, g_ckv, *, prompt, tm, table_period, tq=None):
    n, d = x.shape
    w_all, wq, wqs, wk, wv, wukbd = pw
    cq_t, sq_t, ck_t, sk_t = tables
    nt = table_period // tm
    row = lambda w: pl.BlockSpec((tm, w), lambda i: (i, 0))
    tab = pl.BlockSpec((tm, LANE), lambda i: (i % nt, 0))
    ins = [x, g_attn.reshape(1, d), w_all, g_cq.reshape(1, -1), g_ckv.reshape(1, -1), wq, wqs, cq_t, sq_t, ck_t, sk_t]
    in_specs = [row(d), _full((1, d)), _full(w_all.shape), _full((1, MLA_Q_RANK)), _full((1, MLA_KV_RANK)),
                _full(wq.shape), _full(wqs.shape), tab, tab, tab, tab]
    outs = [(MLA_KV_RANK + MLA_ROPE, F32, False), (2 * HEAD_DIM, F32, False), (4 * HEAD_DIM, F32, False),
            (2 * HEAD_DIM, F32, False), (LANE, F32, True), (H_SB * HEAD_DIM, BF16, True),
            (H_NSA * HEAD_DIM, BF16, True), (H_MLA * SLOT, BF16, True)]
    if prompt:
        ins += [wk, wv]
        in_specs += [_full(wk.shape), _full(wv.shape)]
        outs += [(H_MLA * SLOT, BF16, False), (H_MLA * MLA_V, BF16, True)] + [(2 * HEAD_DIM, BF16, True)] * 3
    else:
        ins += [wukbd]
        in_specs += [_full(wukbd.shape)]
        outs += [(H_MLA * MLA_KV_RANK, BF16, False)]
    col = lambda w: pl.BlockSpec((tm // tq, w, tq), lambda i: (i, 0, 0))
    return pl.pallas_call(
        functools.partial(_proj_kernel, prompt=prompt, tq=tq),
        grid=(n // tm,),
        in_specs=in_specs,
        out_specs=[col(w) if (prompt and tr) else row(w) for w, _, tr in outs],
        out_shape=[jax.ShapeDtypeStruct((n // tq, w, tq) if (prompt and tr) else (n, w), dt) for w, dt, tr in outs],
        compiler_params=_cparams(("parallel",)),
        name="proj_prompt" if prompt else "proj_sample",
    )(*ins)


def _finish_kernel(x_ref, oa_ref, ob_ref, oc_ref, p_ref, ga_ref, gb_ref, gc_ref, woa_ref, wob_ref, woc_ref,
                   gffn_ref, wg_ref, wu_ref, wd_ref, gple_ref, wpg_ref, wple_ref, gfin_ref, out_ref,
                   *, final, n_chunk):
    mix = (_dot(_rms(oa_ref[...], ga_ref[...]).astype(BF16), woa_ref[...])
           + _dot(_rms(ob_ref[...], gb_ref[...]).astype(BF16), wob_ref[...])
           + _dot(_rms(oc_ref[...], gc_ref[...]).astype(BF16), woc_ref[...]))
    x1 = x_ref[...] + mix
    h = _rms(x1, gffn_ref[...]).astype(BF16)
    fc = wg_ref.shape[1] // n_chunk
    ff = jnp.zeros_like(x1)
    for c in range(n_chunk):
        g = _dot(h, wg_ref[:, c * fc:(c + 1) * fc])
        u = _dot(h, wu_ref[:, c * fc:(c + 1) * fc])
        ff = ff + _dot((g * jax.nn.sigmoid(g) * u).astype(BF16), wd_ref[c * fc:(c + 1) * fc, :])
    x2 = x1 + ff
    pg = jax.nn.sigmoid(_dot(_rms(x2, gple_ref[...]).astype(BF16), wpg_ref[...]))
    x3 = x2 + pg * _dot(p_ref[...].astype(BF16), wple_ref[...])
    out_ref[...] = _rms(x3, gfin_ref[...]) if final else x3


def _finish(x, oa, ob, oc, p, fw, *, final, tm):
    n, d = x.shape
    (ga, gb, gc, woa, wob, woc, gffn, wg, wu, wd, gple, wpg, wple, gfin) = fw
    row = lambda w: pl.BlockSpec((tm, w), lambda i: (i, 0))
    const = lambda a: pl.BlockSpec(a.shape, lambda i: (0,) * a.ndim, pipeline_mode=pl.Buffered(1))
    ws = [ga, gb, gc, woa, wob, woc, gffn, wg, wu, wd, gple, wpg, wple, gfin]
    d_ff = wg.shape[1]
    n_chunk = 2 if d_ff % (2 * LANE) == 0 else 1
    return pl.pallas_call(
        functools.partial(_finish_kernel, final=final, n_chunk=n_chunk),
        grid=(n // tm,),
        in_specs=[row(d), row(oa.shape[1]), row(ob.shape[1]), row(oc.shape[1]), row(p.shape[1])] + [const(w) for w in ws],
        out_specs=row(d),
        out_shape=jax.ShapeDtypeStruct((n, d), F32),
        compiler_params=_cparams(("parallel",)),
        name="finish",
    )(x, oa, ob, oc, p, *ws)


def _finish_weights(g_grp_mla, g_grp_sb, g_grp_nsa, w_out, g_ffn, w_gate, w_up, w_down, g_ple, w_ple_gate, w_ple,
                    g_final):
    na, nb = H_MLA * MLA_V, H_SB * HEAD_DIM
    r = lambda g: g.reshape(1, -1)
    return (r(g_grp_mla), r(g_grp_sb), r(g_grp_nsa), w_out[:na].astype(BF16), w_out[na:na + nb].astype(BF16),
            w_out[na + nb:].astype(BF16), r(g_ffn), w_gate.astype(BF16), w_up.astype(BF16), w_down.astype(BF16),
            r(g_ple), w_ple_gate.astype(BF16), w_ple.astype(BF16), r(g_final))


def _softmax_step(s, mask, m, l):
    if mask is not None:
        s = jnp.where(mask, s, NEG_BIG)
    m_new = jnp.maximum(m, jnp.max(s, axis=1, keepdims=True))
    p = jnp.exp(s - m_new)
    if mask is not None:
        p = jnp.where(mask, p, 0.0)
    alpha = jnp.exp(m - m_new)
    return p, alpha, m_new, alpha * l + jnp.sum(p, axis=1, keepdims=True)


def _softmax_init(m_sc, l_sc, acc_sc):
    m_sc[...] = jnp.full(m_sc.shape, NEG_BIG, F32)
    l_sc[...] = jnp.zeros(l_sc.shape, F32)
    acc_sc[...] = jnp.zeros(acc_sc.shape, F32)


def _softmax_update_t(ss, mask, v_ts, m_sc, l_sc, acc_sc):
    heads = range(len(ss))
    if mask is not None:
        ss = [jnp.where(mask, s, NEG_BIG) for s in ss]
    m_old = [m_sc[h] for h in heads]
    m_new = [jnp.maximum(m_old[h], jnp.max(ss[h], axis=0, keepdims=True)) for h in heads]
    ps = [jnp.exp(ss[h] - m_new[h]) for h in heads]
    if mask is not None:
        ps = [jnp.where(mask, p, 0.0) for p in ps]
    alpha = [jnp.exp(m_old[h] - m_new[h]) for h in heads]
    pv = [_dot(v_ts[h], ps[h].astype(BF16)) for h in heads]
    for h in heads:
        m_sc[h] = m_new[h]
        l_sc[h] = alpha[h] * l_sc[h] + jnp.sum(ps[h], axis=0, keepdims=True)
        acc_sc[h] = alpha[h] * acc_sc[h] + pv[h]


def _softmax_result_t(l_sc, acc_sc, n_head):
    o_t = jnp.concatenate([acc_sc[h] / jnp.maximum(l_sc[h], 1e-30) for h in range(n_head)], axis=0)
    return o_t.T


def _tile_iotas(tq):
    key = lax.broadcasted_iota(jnp.int32, (tq, 1), 0)
    qry = lax.broadcasted_iota(jnp.int32, (1, tq), 1)
    return key, qry


def _mla_prompt_kernel(qt_ref, k_ref, vt_ref, o_ref, m_sc, l_sc, acc_sc, *, tq):
    qi = pl.program_id(1)
    _softmax_init(m_sc, l_sc, acc_sc)

    def tile(kt, mask):
        k0 = pl.multiple_of(kt * tq, tq)
        ss = [_dot(k_ref[pl.ds(k0, tq), h * SLOT:(h + 1) * SLOT], qt_ref[h * SLOT:(h + 1) * SLOT, :]) * MLA_SCALE
              for h in range(H_MLA)]
        v_ts = [vt_ref[kt, h * MLA_V:(h + 1) * MLA_V, :] for h in range(H_MLA)]
        _softmax_update_t(ss, mask, v_ts, m_sc, l_sc, acc_sc)

    def body(kt, carry):
        tile(kt, None)
        return carry

    lax.fori_loop(0, qi, body, 0)
    key, qry = _tile_iotas(tq)
    tile(qi, key <= qry)
    o_ref[...] = _softmax_result_t(l_sc, acc_sc, H_MLA)


def _mla_prompt(qa_t, ka, va_t, *, b, t, tq):
    nq = t // tq
    return pl.pallas_call(
        functools.partial(_mla_prompt_kernel, tq=tq),
        grid=(b, nq),
        in_specs=[pl.BlockSpec((None, H_MLA * SLOT, tq), lambda i, j: (i * nq + j, 0, 0)),
                  pl.BlockSpec((t, H_MLA * SLOT), lambda i, j: (i, 0)),
                  pl.BlockSpec((nq, H_MLA * MLA_V, tq), lambda i, j: (i, 0, 0))],
        out_specs=pl.BlockSpec((tq, H_MLA * MLA_V), lambda i, j: (i * nq + j, 0)),
        out_shape=jax.ShapeDtypeStruct((b * t, H_MLA * MLA_V), F32),
        scratch_shapes=[pltpu.VMEM((H_MLA, 1, tq), F32), pltpu.VMEM((H_MLA, 1, tq), F32),
                        pltpu.VMEM((H_MLA, MLA_V, tq), F32)],
        compiler_params=_cparams(("parallel", "arbitrary")),
        name="mla_prompt",
    )(qa_t, ka, va_t)


def _sb_tile(z, mask, r_sum, v, u_tri):
    sp = jnp.log(1.0 + jnp.exp(-jnp.abs(z)))
    log_keep = -jnp.maximum(z, 0.0) - sp
    log_beta = jnp.minimum(z, 0.0) - sp
    if mask is not None:
        log_keep = jnp.where(mask, log_keep, 0.0)
    excl = _dot_hilo(log_keep, u_tri)
    a = jnp.exp(log_beta + excl + r_sum)
    if mask is not None:
        a = jnp.where(mask, a, 0.0)
    return _dot(a.astype(BF16), v), r_sum + jnp.sum(log_keep, axis=1, keepdims=True)


def _sb_prompt_kernel(qt_ref, k_ref, kvt_ref, ut_ref, o_ref, r_sc, acc_sc, *, tq):
    qi = pl.program_id(1)
    d = HEAD_DIM
    r_sc[...] = jnp.zeros(r_sc.shape, F32)
    acc_sc[...] = jnp.zeros(acc_sc.shape, F32)
    q_t = jnp.concatenate([qt_ref[h * d:(h + 1) * d, :] for h in range(H_SB)], axis=1)
    ut = ut_ref[...]

    def tile(kt, mask):
        k0 = pl.multiple_of(kt * tq, tq)
        z = _dot(k_ref[pl.ds(k0, tq), 0:d].astype(BF16), q_t) * ATT_SCALE
        sp = jnp.log(1.0 + jnp.exp(-jnp.abs(z)))
        log_keep = -jnp.maximum(z, 0.0) - sp
        if mask is not None:
            log_keep = jnp.where(mask, log_keep, 0.0)
        keep_hi = log_keep.astype(BF16)
        keep_lo = (log_keep - keep_hi.astype(F32)).astype(BF16)
        excl = _dot(ut, keep_hi) + _dot(ut, keep_lo)
        a = jnp.exp(jnp.minimum(z, 0.0) - sp + excl + r_sc[...])
        if mask is not None:
            a = jnp.where(mask, a, 0.0)
        acc_sc[...] = acc_sc[...] + _dot(kvt_ref[kt, d:2 * d, :], a.astype(BF16))
        r_sc[...] = r_sc[...] + jnp.sum(log_keep, axis=0, keepdims=True)

    key = lax.broadcasted_iota(jnp.int32, (tq, 1), 0)
    qry = lax.broadcasted_iota(jnp.int32, (1, H_SB * tq), 1) & (tq - 1)
    tile(qi, key < qry)

    def body(i, carry):
        tile(qi - 1 - i, None)
        return carry

    lax.fori_loop(0, qi, body, 0)
    acc = acc_sc[...]
    o_ref[...] = jnp.concatenate([acc[:, h * tq:(h + 1) * tq] for h in range(H_SB)], axis=0).T


def _sb_prompt(sbq_t, sb_rows, sb_t, ut, *, b, t, tq):
    nq = t // tq
    assert tq & (tq - 1) == 0
    return pl.pallas_call(
        functools.partial(_sb_prompt_kernel, tq=tq),
        grid=(b, nq),
        in_specs=[pl.BlockSpec((None, H_SB * HEAD_DIM, tq), lambda i, j: (i * nq + j, 0, 0)),
                  pl.BlockSpec((t, 2 * HEAD_DIM), lambda i, j: (i, 0)),
                  pl.BlockSpec((nq, 2 * HEAD_DIM, tq), lambda i, j: (i, 0, 0)),
                  _full(ut.shape)],
        out_specs=pl.BlockSpec((tq, H_SB * HEAD_DIM), lambda i, j: (i * nq + j, 0)),
        out_shape=jax.ShapeDtypeStruct((b * t, H_SB * HEAD_DIM), F32),
        scratch_shapes=[pltpu.VMEM((1, H_SB * tq), F32), pltpu.VMEM((HEAD_DIM, H_SB * tq), F32)],
        compiler_params=_cparams(("parallel", "arbitrary")),
        name="sb_prompt",
    )(sbq_t, sb_rows, sb_t, ut)


def _compress_weights(pe_k, w1_k, pe_v, w1_v):
    d = HEAD_DIM
    w1k = w1_k.reshape(CMP_BLOCK, d, -1)
    w1v = w1_v.reshape(CMP_BLOCK, d, -1)
    z = jnp.zeros((d, d), F32)
    mats, pes = [], []
    for i in range(CMP_STRIDE // 2):
        blocks = []
        for r in (2 * i, 2 * i + 1):
            blocks.append(jnp.concatenate([w1k[r], w1k[CMP_STRIDE + r], z, z], axis=1))
            blocks.append(jnp.concatenate([z, z, w1v[r], w1v[CMP_STRIDE + r]], axis=1))
        mats.append(jnp.concatenate(blocks, axis=0))
        row_a = jnp.concatenate([pe_k[2 * i], pe_v[2 * i], pe_k[2 * i + 1], pe_v[2 * i + 1]])
        row_b = jnp.concatenate([pe_k[CMP_STRIDE + 2 * i], pe_v[CMP_STRIDE + 2 * i],
                                 pe_k[CMP_STRIDE + 2 * i + 1], pe_v[CMP_STRIDE + 2 * i + 1]])
        pes.append(jnp.concatenate([row_a[None], row_b[None], jnp.zeros((6, 4 * d), F32)], axis=0))
    return jnp.stack(mats).astype(BF16), jnp.stack(pes).astype(BF16)


def _compress_bias(pe_ref, wc_ref):
    acc = jnp.zeros((8, 4 * HEAD_DIM), F32)
    for i in range(CMP_STRIDE // 2):
        acc = acc + _dot(pe_ref[i], wc_ref[i])
    lane = lax.broadcasted_iota(jnp.int32, (1, 4 * HEAD_DIM), 1)
    is_a = (lane & HEAD_DIM) == 0
    return jnp.where(is_a, acc[0:1, :], acc[1:2, :])


def _compress_finish(ab, w2k_ref, w2v_ref):
    n_sub = ab.shape[0]
    nxt = pltpu.roll(ab, n_sub - 1, axis=0)
    d = HEAD_DIM
    hid_k = _gelu_tanh(ab[:, 0:d] + nxt[:, d:2 * d])
    hid_v = _gelu_tanh(ab[:, 2 * d:3 * d] + nxt[:, 3 * d:4 * d])
    kc = _dot(hid_k.astype(BF16), w2k_ref[...]).astype(BF16)
    vc = _dot(hid_v.astype(BF16), w2v_ref[...]).astype(BF16)
    return kc, vc


def _select_blocks_t(imp, cur, n_blk):
    blk = lax.broadcasted_iota(jnp.int32, (n_blk, 1), 0)
    visible = blk <= cur
    forced = (blk == 0) | (blk == cur) | (blk == cur - 1)
    score = jnp.where(visible, jnp.where(forced, FORCE_SCORE, imp), -1.0)
    rank = jnp.zeros(imp.shape, jnp.int32)
    for b2 in range(n_blk):
        sb2 = score[b2:b2 + 1, :]
        ahead = (sb2 > score) | ((sb2 == score) & (b2 < blk))
        rank = rank + ahead.astype(jnp.int32)
    return ((rank < SEL_TOP_N) & (score >= 0.0)).astype(F32)


def _nsa_prompt_kernel(qt_ref, cmp_ref, slc_ref, slct_ref, win_ref, wint_ref, gatet_ref, wc_ref, pe_ref, w2k_ref,
                       w2v_ref, ovt_ref, eselt_ref, bct_ref, bdt_ref, c31_ref, o_ref,
                       kc_sc, vct_sc, ms_sc, ls_sc, accs_sc, mw_sc, lw_sc, accw_sc, *, tq, n_sub, n_blk):
    qi = pl.program_id(1)
    d = HEAD_DIM

    @pl.when(qi == 0)
    def _():
        ab = jnp.zeros((n_sub, 4 * d), F32)
        for i in range(CMP_STRIDE // 2):
            lhs = jnp.concatenate([cmp_ref[pl.ds(2 * i, n_sub, stride=CMP_STRIDE), :],
                                   cmp_ref[pl.ds(2 * i + 1, n_sub, stride=CMP_STRIDE), :]], axis=1)
            ab = ab + _dot(lhs.astype(BF16), wc_ref[i])
        kc, vc_wide = _compress_finish(ab + _compress_bias(pe_ref, wc_ref), w2k_ref, w2v_ref)
        kc_sc[...] = kc
        vct_sc[...] = vc_wide.astype(F32).T[0:d, :].astype(BF16)

    key, qry = _tile_iotas(tq)
    qpos = qi * tq + qry
    kc, vc_t = kc_sc[...], vct_sc[...]
    c_end = lax.broadcasted_iota(jnp.int32, (n_sub, 1), 0) * CMP_STRIDE + (CMP_BLOCK - 1)
    mask_c = c_end <= qpos

    heads = range(H_NSA)
    ss = [jnp.where(mask_c, _dot(kc, qt_ref[h * d:(h + 1) * d, :]) * ATT_SCALE + bct_ref[h], NEG_BIG) for h in heads]
    es = [jnp.where(mask_c, jnp.exp(s - jnp.max(s, axis=0, keepdims=True)), 0.0) for s in ss]
    ps = [e / jnp.maximum(jnp.sum(e, axis=0, keepdims=True), 1e-30) for e in es]
    o_cmp = [_dot(vc_t, p.astype(BF16)) for p in ps]
    psum = ps[0]
    for p in ps[1:]:
        psum = psum + p
    p_hi = psum.astype(BF16)
    p_lo = (psum - p_hi.astype(F32)).astype(BF16)
    imp = _dot(ovt_ref[...], p_hi) + _dot(ovt_ref[...], p_lo)
    sel = _select_blocks_t(imp, qpos >> 6, n_blk).astype(BF16)

    slc_state = (ms_sc, ls_sc, accs_sc)
    win_state = (mw_sc, lw_sc, accw_sc)
    _softmax_init(*slc_state)
    _softmax_init(*win_state)

    def attend(k_ref, vt_ref, kt, delta, mask, state):
        k0 = pl.multiple_of(kt * tq, tq)
        k = k_ref[pl.ds(k0, tq), 0:d].astype(BF16)
        v_t = vt_ref[kt, d:2 * d, :]
        ss = [_dot(k, qt_ref[h * d:(h + 1) * d, :]) * ATT_SCALE
              + (c31_ref[h] if (delta is None or delta >= 2) else bdt_ref[delta, h]) for h in range(H_NSA)]
        _softmax_update_t(ss, mask, [v_t] * H_NSA, *state)

    def picked(kt):
        return _dot(eselt_ref[kt], sel) > 0.5

    def far_body(kt, carry):
        attend(slc_ref, slct_ref, kt, None, picked(kt), slc_state)
        return carry

    lax.fori_loop(0, jnp.maximum(qi - 1, 0), far_body, 0)

    @pl.when(qi >= 1)
    def _():
        attend(slc_ref, slct_ref, qi - 1, 1, picked(qi - 1), slc_state)

    attend(slc_ref, slct_ref, qi, 0, picked(qi) & (key <= qry), slc_state)

    for delta in reversed(range((WINDOW + tq - 2) // tq + 1)):
        mask = None
        if delta == 0:
            mask = key <= qry
        elif (delta + 1) * tq - 1 >= WINDOW:
            mask = (delta * tq + qry - key) < WINDOW

        @pl.when(qi >= delta)
        def _(delta=delta, mask=mask):
            attend(win_ref, wint_ref, qi - delta, delta, mask, win_state)

    gates = gatet_ref[...]
    mixed = []
    for h in range(H_NSA):
        o_slc = accs_sc[h] / jnp.maximum(ls_sc[h], 1e-30)
        o_win = accw_sc[h] / jnp.maximum(lw_sc[h], 1e-30)
        mixed.append(gates[3 * h:3 * h + 1, :] * o_cmp[h] + gates[3 * h + 1:3 * h + 2, :] * o_slc
                     + gates[3 * h + 2:3 * h + 3, :] * o_win)
    o_ref[...] = jnp.concatenate(mixed, axis=0).T


def _nsa_prompt(nq_t, nsa_rows, slc_t, win_rows, win_t, gates_t, cw, tabs, *, b, t, tq):
    nqb = t // tq
    n_sub = t // CMP_STRIDE
    n_blk = -(-t // SEL_BLOCK)
    wc, pe, w2k, w2v = cw
    ov_t, esel_t, bias_ct, bias_dt, c31 = tabs
    d = HEAD_DIM
    w2v_wide = jnp.pad(w2v, ((0, 0), (0, LANE - d)))
    tile3 = lambda w: pl.BlockSpec((nqb, w, tq), lambda i, j: (i, 0, 0))
    return pl.pallas_call(
        functools.partial(_nsa_prompt_kernel, tq=tq, n_sub=n_sub, n_blk=n_blk),
        grid=(b, nqb),
        in_specs=[pl.BlockSpec((None, H_NSA * d, tq), lambda i, j: (i * nqb + j, 0, 0)),
                  pl.BlockSpec((t, 2 * d), lambda i, j: (i, 0)),
                  pl.BlockSpec((t, 2 * d), lambda i, j: (i, 1)),
                  tile3(2 * d),
                  pl.BlockSpec((t, 2 * d), lambda i, j: (i, 0)),
                  tile3(2 * d),
                  pl.BlockSpec((None, LANE, tq), lambda i, j: (i * nqb + j, 0, 0)),
                  _full(wc.shape), _full(pe.shape), _full(w2k.shape), _full(w2v_wide.shape),
                  _full(ov_t.shape), _full(esel_t.shape),
                  pl.BlockSpec((H_NSA, n_sub, tq), lambda i, j: (0, 0, j)),
                  _full(bias_dt.shape), _full(c31.shape)],
        out_specs=pl.BlockSpec((tq, H_NSA * d), lambda i, j: (i * nqb + j, 0)),
        out_shape=jax.ShapeDtypeStruct((b * t, H_NSA * d), F32),
        scratch_shapes=[pltpu.VMEM((n_sub, d), BF16), pltpu.VMEM((d, n_sub), BF16)]
        + [pltpu.VMEM((H_NSA, 1, tq), F32), pltpu.VMEM((H_NSA, 1, tq), F32), pltpu.VMEM((H_NSA, d, tq), F32)] * 2,
        compiler_params=_cparams(("parallel", "arbitrary")),
        name="nsa_prompt",
    )(nq_t, nsa_rows, nsa_rows, slc_t, win_rows, win_t, gates_t, wc, pe, w2k, w2v_wide, ov_t, esel_t, bias_ct,
      bias_dt, c31)


def _overlap_matrix(n_cmp_pad, n_blk_pad):
    c_start = np.arange(n_cmp_pad)[:, None] * CMP_STRIDE
    b_start = np.arange(n_blk_pad)[None, :] * SEL_BLOCK
    return ((c_start < b_start + SEL_BLOCK) & (c_start + CMP_BLOCK > b_start)).astype(np.float32)


def _nsa_prompt_tables(rel_bias, t, tq):
    n_sub = t // CMP_STRIDE
    n_blk = -(-t // SEL_BLOCK)
    ov_t = jnp.asarray(_overlap_matrix(n_sub, n_blk).T, BF16)
    key_blk = np.arange(t) // SEL_BLOCK
    esel = (key_blk[:, None] == np.arange(n_blk)[None, :]).astype(np.float32)
    esel_t = jnp.asarray(esel.reshape(t // tq, tq, n_blk), BF16)
    qpos = np.arange(t)[None, :]
    dist_c = qpos - (np.arange(n_sub)[:, None] * CMP_STRIDE + CMP_BLOCK - 1)
    bias_ct = jnp.transpose(_bias_lookup(rel_bias, dist_c), (2, 0, 1))
    key = np.arange(tq)[:, None]
    qry = np.arange(tq)[None, :]
    dist_d = np.stack([qry - key, tq + qry - key])
    bias_dt = jnp.transpose(_bias_lookup(rel_bias, dist_d), (0, 3, 1, 2))
    c31 = rel_bias[N_BUCKETS - 1].reshape(H_NSA, 1, 1).astype(F32)
    return ov_t, esel_t, bias_ct, bias_dt, c31


def _sample_paged_kernel(pt_ref, qabs_ref, mnew_ref, sbq_ref, sbnew_ref, nq_ref, wnew_ref, wbuf_ref,
                         cmla_ref, csb_ref, cnsa_ref,
                         wuv_ref, wc_ref, pe_ref, w2k_ref, w2v_ref, u_ref, ov_ref, gq_ref, bc_ref, bw1_ref, bw2_ref,
                         csum_ref, cexp_ref,
                         oa_ref, ob_ref, ocmp_ref, owin_ref, idx_ref,
                         mla_buf, sb_buf, nsa_buf, sem, m_sc, l_sc, acc_sc, r_sc, sbacc_sc, ab_sc,
                         *, layer, n_pg, n_sub, n_blk_pad, past, nq_tok):
    b = pl.program_id(0)
    j = pl.program_id(1)
    n_seq = pl.num_programs(0)
    n_step = pl.num_programs(1)
    d = HEAD_DIM
    c = MLA_KV_RANK

    def page_copies(bb, jj, slot):
        out = []
        for i in range(n_pg):
            pg = pt_ref[bb, (n_step - 1 - jj) * n_pg + i]
            rows = pl.ds(i * LANE, LANE)
            out.append(pltpu.make_async_copy(cmla_ref.at[layer, pg], mla_buf.at[slot, :, rows], sem.at[slot, 0]))
            out.append(pltpu.make_async_copy(csb_ref.at[layer, pg], sb_buf.at[slot, rows], sem.at[slot, 1]))
            out.append(pltpu.make_async_copy(cnsa_ref.at[layer, pg, :, pl.ds(0, 2 * d)], nsa_buf.at[slot, rows],
                                             sem.at[slot, 2]))
        return out

    step = b * n_step + j
    slot = step & 1

    @pl.when(step == 0)
    def _():
        for cp in page_copies(b, j, slot):
            cp.start()

    @pl.when(step + 1 < n_seq * n_step)
    def _():
        wrap = j + 1 == n_step
        for cp in page_copies(jnp.where(wrap, b + 1, b), jnp.where(wrap, 0, j + 1), 1 - slot):
            cp.start()

    for cp in page_copies(b, j, slot):
        cp.wait()
    q_mla = qabs_ref[...]
    q_sb = sbq_ref[...]
    u_tri = u_ref[...]
    row_q = lax.broadcasted_iota(jnp.int32, (ROWS_Q, 1), 0) >> 3
    row_q_sb = lax.broadcasted_iota(jnp.int32, (nq_tok * H_SB, 1), 0) >> 2
    new_col = lax.broadcasted_iota(jnp.int32, (1, NEW_PAD), 1)

    @pl.when(j == 0)
    def _():
        mnew = mnew_ref[...].astype(BF16)
        s = _dot_nt(q_mla, mnew) * MLA_SCALE
        p, _, m, l = _softmax_step(s, new_col <= row_q, jnp.full((ROWS_Q, 1), NEG_BIG, F32),
                                   jnp.zeros((ROWS_Q, 1), F32))
        m_sc[...] = m
        l_sc[...] = l
        acc_sc[...] = _dot(p.astype(BF16), mnew[:, :c])
        sbnew = sbnew_ref[...].astype(BF16)
        z = _dot_nt(q_sb, sbnew[:, :d]) * ATT_SCALE
        o, r_sum = _sb_tile(z, new_col < row_q_sb, jnp.zeros((nq_tok * H_SB, 1), F32), sbnew[:, d:],
                            u_tri[:NEW_PAD, :NEW_PAD])
        sbacc_sc[...] = o
        r_sc[...] = r_sum

    n_chunk = n_pg * LANE // SB_CHUNK
    n_row = nq_tok * H_SB
    keys_t = mla_buf[slot].astype(BF16)
    kv = sb_buf[slot].astype(BF16)
    s = _dot(q_mla, keys_t) * MLA_SCALE
    z = _dot_nt(q_sb, kv[:, :d]) * ATT_SCALE
    ab = jnp.zeros((n_pg * 8, 4 * d), F32)
    for i2 in range(CMP_STRIDE // 2):
        lhs = jnp.concatenate([nsa_buf[slot, pl.ds(2 * i2, n_pg * 8, stride=CMP_STRIDE), :],
                               nsa_buf[slot, pl.ds(2 * i2 + 1, n_pg * 8, stride=CMP_STRIDE), :]], axis=1)
        ab = ab + _dot(lhs.astype(BF16), wc_ref[i2])
    sub0 = pl.multiple_of((n_step - 1 - j) * (n_pg * 8), n_pg * 8)
    ab_sc[pl.ds(sub0, n_pg * 8), :] = ab

    m_old = m_sc[...]
    m_new = jnp.maximum(m_old, jnp.max(s, axis=1, keepdims=True))
    sp = jnp.log(1.0 + jnp.exp(-jnp.abs(z)))
    log_keep = -jnp.maximum(z, 0.0) - sp
    keep_hi = log_keep.astype(BF16)
    keep_lo = (log_keep - keep_hi.astype(F32)).astype(BF16)
    keep2 = jnp.concatenate([keep_hi, keep_lo], axis=0)
    p = jnp.exp(s - m_new)
    alpha = jnp.exp(m_old - m_new)

    e2 = [_dot(keep2[:, ch * SB_CHUNK:(ch + 1) * SB_CHUNK], u_tri) for ch in range(n_chunk)]
    c2 = _dot(keep2, csum_ref[...])
    pv = _dot_nt(p.astype(BF16), keys_t[:c, :])
    excl = jnp.concatenate([e[:n_row] + e[n_row:] for e in e2], axis=1)
    csum = c2[:n_row] + c2[n_row:]
    right = _dot_hilo(csum, cexp_ref[...])
    m_sc[...] = m_new
    l_sc[...] = alpha * l_sc[...] + jnp.sum(p, axis=1, keepdims=True)
    acc_sc[...] = alpha * acc_sc[...] + pv
    a = jnp.exp(jnp.minimum(z, 0.0) - sp + excl + right + r_sc[...])
    sbacc_sc[...] = sbacc_sc[...] + _dot(a.astype(BF16), kv[:, d:])
    r_sc[...] = r_sc[...] + csum[:, n_chunk:n_chunk + 1]

    @pl.when(j == n_step - 1)
    def _():
        o_lat = (acc_sc[...] / jnp.maximum(l_sc[...], 1e-30)).astype(BF16)
        res = _dot(o_lat, wuv_ref[...])
        head = lax.broadcasted_iota(jnp.int32, (ROWS_Q, 1), 0) & 7
        oa = jnp.zeros((ROWS_Q, MLA_V), F32)
        for h in range(H_MLA):
            oa = oa + jnp.where(head == h, res[:, h * MLA_V:(h + 1) * MLA_V], 0.0)
        oa_ref[...] = oa
        ob_ref[...] = sbacc_sc[...]

        kc, vc = _compress_finish(ab_sc[...] + _compress_bias(pe_ref, wc_ref), w2k_ref, w2v_ref)
        q_n = nq_ref[...]
        c_idx = lax.broadcasted_iota(jnp.int32, (1, n_sub), 1)
        mask_c = (c_idx * CMP_STRIDE + (CMP_BLOCK - 1)) <= (past + row_q)
        s_c = jnp.where(mask_c, _dot_nt(q_n, kc) * ATT_SCALE + bc_ref[...], NEG_BIG)
        e = jnp.where(mask_c, jnp.exp(s_c - jnp.max(s_c, axis=1, keepdims=True)), 0.0)
        p_c = e / jnp.maximum(jnp.sum(e, axis=1, keepdims=True), 1e-30)
        ocmp_ref[...] = _dot(p_c.astype(BF16), vc)
        p_hi = p_c.astype(BF16)
        p_lo = (p_c - p_hi.astype(F32)).astype(BF16)
        psum = _dot(gq_ref[...], p_hi) + _dot(gq_ref[...], p_lo)
        imp = _dot_hilo(psum, ov_ref[...])

        blk = lax.broadcasted_iota(jnp.int32, (8, n_blk_pad), 1)
        blk_f = blk.astype(F32)
        cur = (past + lax.broadcasted_iota(jnp.int32, (8, 1), 0)) >> 6
        forced = (blk == 0) | (blk == cur) | (blk == cur - 1)
        score = jnp.where(blk <= cur, jnp.where(forced, FORCE_SCORE, imp), -1.0)
        lane = lax.broadcasted_iota(jnp.int32, (8, LANE), 1)
        picked = jnp.full((8, LANE), -1.0, F32)
        for t in range(SEL_TOP_N):
            best = jnp.max(score, axis=1, keepdims=True)
            arg = jnp.min(jnp.where(score == best, blk_f, float(n_blk_pad)), axis=1, keepdims=True)
            picked = jnp.where(lane == t, jnp.where(best >= 0.0, arg, -1.0), picked)
            score = jnp.where(blk_f == arg, -3e38, score)
        idx_ref[...] = picked.astype(jnp.int32)

        wbuf = wbuf_ref[...].astype(BF16)
        wnew = wnew_ref[...].astype(BF16)
        n_buf = wbuf.shape[0]
        buf_col = lax.broadcasted_iota(jnp.int32, (1, n_buf), 1)
        dist1 = n_buf + row_q - buf_col
        mask1 = (dist1 >= 0) & (dist1 < WINDOW)
        mask2 = new_col <= row_q
        s1 = jnp.where(mask1, _dot_nt(q_n, wbuf[:, :d]) * ATT_SCALE + bw1_ref[...], NEG_BIG)
        s2 = jnp.where(mask2, _dot_nt(q_n, wnew[:, :d]) * ATT_SCALE + bw2_ref[...], NEG_BIG)
        m_w = jnp.maximum(jnp.max(s1, axis=1, keepdims=True), jnp.max(s2, axis=1, keepdims=True))
        e1 = jnp.where(mask1, jnp.exp(s1 - m_w), 0.0)
        e2 = jnp.where(mask2, jnp.exp(s2 - m_w), 0.0)
        l_w = jnp.sum(e1, axis=1, keepdims=True) + jnp.sum(e2, axis=1, keepdims=True)
        o_w = _dot(e1.astype(BF16), wbuf[:, d:]) + _dot(e2.astype(BF16), wnew[:, d:])
        owin_ref[...] = o_w / jnp.maximum(l_w, 1e-30)


def _sample_paged(layer, page_table, qabs, mla_new, sbq, sb_new, nq, win_new, win_buf,
                  cache_mla, cache_sb, cache_nsa, consts, *, past):
    nb, n_pages = page_table.shape
    n_pg = min(PAGES_PER_STEP, n_pages)
    n_step = n_pages // n_pg
    n_sub = past // CMP_STRIDE
    nq_tok = sbq.shape[1] // H_SB
    wuv, wc, pe, w2k, w2v, u_tri, ov, gq, bias_c, bias_w1, bias_w2 = consts
    n_blk_pad = ov.shape[1]
    d = HEAD_DIM

    def seq(a):
        return pl.BlockSpec((None,) + a.shape[1:], lambda b, j, pt: (b,) + (0,) * (a.ndim - 1))

    cst = lambda a: pl.BlockSpec(a.shape, lambda b, j, pt: (0,) * a.ndim)
    seq_in = [qabs, mla_new, sbq, sb_new, nq, win_new, win_buf]
    n_chunk = n_pg * LANE // SB_CHUNK
    key_chunk = np.arange(n_pg * LANE) // SB_CHUNK
    lane_c = np.arange(LANE)
    csum_m = ((key_chunk[:, None] > lane_c[None, :]) & (lane_c[None, :] < n_chunk)) | (lane_c[None, :] == n_chunk)
    cexp_m = (lane_c[:, None] == key_chunk[None, :])
    const_in = [wuv, wc, pe, w2k, w2v, u_tri, ov, gq, bias_c, bias_w1, bias_w2,
                jnp.asarray(csum_m, BF16), jnp.asarray(cexp_m, BF16)]
    in_specs = ([seq(a) for a in seq_in] + [pl.BlockSpec(memory_space=pl.ANY)] * 3 + [cst(a) for a in const_in])
    out_shape = [jax.ShapeDtypeStruct((nb, ROWS_Q, MLA_V), F32),
                 jax.ShapeDtypeStruct((nb, nq_tok * H_SB, d), F32),
                 jax.ShapeDtypeStruct((nb, ROWS_Q, d), F32),
                 jax.ShapeDtypeStruct((nb, ROWS_Q, d), F32),
                 jax.ShapeDtypeStruct((nb, 8, LANE), jnp.int32)]
    out_specs = [pl.BlockSpec((None,) + s.shape[1:], lambda b, j, pt: (b, 0, 0)) for s in out_shape]
    kernel = functools.partial(_sample_paged_kernel, layer=layer, n_pg=n_pg, n_sub=n_sub, n_blk_pad=n_blk_pad,
                               past=past, nq_tok=nq_tok)
    return pl.pallas_call(
        kernel,
        grid_spec=pltpu.PrefetchScalarGridSpec(
            num_scalar_prefetch=1,
            grid=(nb, n_step),
            in_specs=in_specs,
            out_specs=out_specs,
            scratch_shapes=[pltpu.VMEM((2, MLA_KV_RANK + MLA_ROPE, n_pg * LANE), F32),
                            pltpu.VMEM((2, n_pg * LANE, 2 * d), F32),
                            pltpu.VMEM((2, n_pg * LANE, 2 * d), F32),
                            pltpu.SemaphoreType.DMA((2, 3)),
                            pltpu.VMEM((ROWS_Q, 1), F32), pltpu.VMEM((ROWS_Q, 1), F32),
                            pltpu.VMEM((ROWS_Q, MLA_KV_RANK), F32),
                            pltpu.VMEM((nq_tok * H_SB, 1), F32), pltpu.VMEM((nq_tok * H_SB, d), F32),
                            pltpu.VMEM((n_sub, 4 * d), F32)]),
        out_shape=out_shape,
        compiler_params=_cparams(("arbitrary", "arbitrary")),
        name="sample_paged",
    )(page_table, *seq_in, jnp.swapaxes(cache_mla, 2, 3), cache_sb, cache_nsa, *const_in)


def _sample_slc_kernel(idx_ref, pt_ref, nq_ref, nnew_ref, ocmp_ref, owin_ref, gate_ref, tsp_ref, c31_ref, cache_ref,
                       o_ref, kv_buf, sem, *, layer, past, nq_tok):
    b = pl.program_id(0)
    d = HEAD_DIM
    n_past_blk = past // SEL_BLOCK
    blk_per_page = LANE // SEL_BLOCK

    cur_slot = b & 1

    def copies(bb, buf_slot, q):
        out = []
        for k in range(SEL_TOP_N):
            blk = jnp.clip(idx_ref[bb, q, k], 0, n_past_blk - 1)
            page = pt_ref[bb, blk // blk_per_page]
            r0 = pl.multiple_of((blk % blk_per_page) * SEL_BLOCK, SEL_BLOCK)
            out.append(pltpu.make_async_copy(
                cache_ref.at[layer, page, pl.ds(r0, SEL_BLOCK), pl.ds(2 * d, 2 * d)], kv_buf.at[buf_slot, q, k],
                sem.at[buf_slot, q]))
        return out

    @pl.when(b == 0)
    def _():
        for q in range(nq_tok):
            for cp in copies(b, cur_slot, q):
                cp.start()

    @pl.when(b + 1 < pl.num_programs(0))
    def _():
        for q in range(nq_tok):
            for cp in copies(b + 1, 1 - cur_slot, q):
                cp.start()

    lane = lax.broadcasted_iota(jnp.int32, (1, SEL_TOP_N * SEL_BLOCK), 1)
    slot = lane >> 6
    new_col = lax.broadcasted_iota(jnp.int32, (1, NEW_PAD), 1)
    nnew = nnew_ref[...].astype(BF16)
    c31 = c31_ref[...]
    for q in range(nq_tok):
        for cp in copies(b, cur_slot, q):
            cp.wait()
        idx_vec = jnp.full(lane.shape, -1, jnp.int32)
        for k in range(SEL_TOP_N):
            idx_vec = jnp.where(slot == k, idx_ref[b, q, k], idx_vec)
        has_new = jnp.max((idx_vec == n_past_blk).astype(jnp.int32), axis=1, keepdims=True) > 0
        kpos = idx_vec * SEL_BLOCK + (lane & (SEL_BLOCK - 1))
        mask = (idx_vec >= 0) & (idx_vec < n_past_blk) & (kpos <= past + q)
        rows = slice(8 * q, 8 * q + 8)
        bias = jnp.broadcast_to(c31[rows, 0:1], (8, SEL_TOP_N * SEL_BLOCK))
        for t in (1, 2):
            bias = jnp.where(idx_vec == n_past_blk - t, tsp_ref[t, rows, :], bias)
        kv = kv_buf[cur_slot, q].reshape(SEL_TOP_N * SEL_BLOCK, 2 * d).astype(BF16)
        qn = nq_ref[rows, :].astype(BF16)
        s1 = jnp.where(mask, _dot_nt(qn, kv[:, :d]) * ATT_SCALE + bias, NEG_BIG)
        mask2 = (new_col <= q) & (new_col < nq_tok) & has_new
        s2 = jnp.where(mask2, _dot_nt(qn, nnew[:, 2 * d:3 * d]) * ATT_SCALE + tsp_ref[0, rows, 0:NEW_PAD], NEG_BIG)
        m = jnp.maximum(jnp.max(s1, axis=1, keepdims=True), jnp.max(s2, axis=1, keepdims=True))
        e1 = jnp.where(mask, jnp.exp(s1 - m), 0.0)
        e2 = jnp.where(mask2, jnp.exp(s2 - m), 0.0)
        l = jnp.sum(e1, axis=1, keepdims=True) + jnp.sum(e2, axis=1, keepdims=True)
        o_slc = (_dot(e1.astype(BF16), kv[:, d:]) + _dot(e2.astype(BF16), nnew[:, 3 * d:])) / jnp.maximum(l, 1e-30)
        g = gate_ref[rows, :]
        o_ref[rows, :] = g[:, 0:1] * ocmp_ref[rows, :] + g[:, 1:2] * o_slc + g[:, 2:3] * owin_ref[rows, :]


def _sample_slc(layer, idx, page_table, nq32, nsa_new, o_cmp, o_win, gates32, tsp, c31, cache_nsa, *, past):
    nb = page_table.shape[0]
    nq_tok = idx.shape[1]
    d = HEAD_DIM
    seq = lambda a: pl.BlockSpec((None,) + a.shape[1:], lambda b, i, p: (b,) + (0,) * (a.ndim - 1))
    cst = lambda a: pl.BlockSpec(a.shape, lambda b, i, p: (0,) * a.ndim)
    return pl.pallas_call(
        functools.partial(_sample_slc_kernel, layer=layer, past=past, nq_tok=nq_tok),
        grid_spec=pltpu.PrefetchScalarGridSpec(
            num_scalar_prefetch=2,
            grid=(nb,),
            in_specs=[seq(nq32), seq(nsa_new), seq(o_cmp), seq(o_win), seq(gates32), cst(tsp), cst(c31),
                      pl.BlockSpec(memory_space=pl.ANY)],
            out_specs=pl.BlockSpec((None, ROWS_Q, d), lambda b, i, p: (b, 0, 0)),
            scratch_shapes=[pltpu.VMEM((2, nq_tok, SEL_TOP_N, SEL_BLOCK, 2 * d), F32),
                            pltpu.SemaphoreType.DMA((2, nq_tok))]),
        out_shape=jax.ShapeDtypeStruct((nb, ROWS_Q, d), F32),
        compiler_params=_cparams(("arbitrary",)),
        name="sample_slc",
    )(idx, page_table, nq32, nsa_new, o_cmp, o_win, gates32, tsp, c31, cache_nsa)


def _sample_tables(rel_bias, past, nq_tok, n_buf):
    n_sub = past // CMP_STRIDE
    n_blk = -(-(past + nq_tok) // SEL_BLOCK)
    n_blk_pad = -(-n_blk // LANE) * LANE
    ov = jnp.asarray(_overlap_matrix(n_sub, n_blk_pad), BF16)
    gq = np.zeros((8, ROWS_Q), np.float32)
    for q in range(nq_tok):
        gq[q, 8 * q:8 * q + H_NSA] = 1.0
    qpos = past + np.arange(nq_tok)[:, None]
    bias_c = _bias_rows(rel_bias, qpos - (np.arange(n_sub)[None, :] * CMP_STRIDE + CMP_BLOCK - 1), 8)
    bias_w1 = _bias_rows(rel_bias, qpos - (past - n_buf + np.arange(n_buf)[None, :]), 8)
    bias_w2 = _bias_rows(rel_bias, qpos - (past + np.arange(NEW_PAD)[None, :]), 8)
    s = np.arange(SEL_BLOCK)[None, :]
    tsp = jnp.stack([jnp.tile(_bias_rows(rel_bias, qpos - (past - t * SEL_BLOCK + s), 8), (1, SEL_TOP_N))
                     for t in range(3)])
    c31 = jnp.pad(jnp.broadcast_to(rel_bias[N_BUCKETS - 1][None, :, None], (nq_tok, H_NSA, LANE)),
                  ((0, 0), (0, 8 - H_NSA), (0, 0))).reshape(ROWS_Q, LANE).astype(F32)
    return ov, jnp.asarray(gq, BF16), bias_c, bias_w1, bias_w2, tsp, c31


def _rows_q8(a, nb, nq_tok, h):
    w = a.shape[1] // h
    a = a.reshape(nb, nq_tok, h, w)
    a = jnp.pad(a, ((0, 0), (0, 0), (0, 8 - h), (0, 0)))
    return a.reshape(nb, nq_tok * 8, w)


def _pad_new(a, nb, nq_tok):
    a = a.reshape(nb, nq_tok, a.shape[-1])
    return jnp.pad(a, ((0, 0), (0, NEW_PAD - nq_tok), (0, 0)))


def kernel(x_prompt, x_sample, cache_mla, cache_sb, cache_nsa, state_win, page_table, p_prompt, p_sample, g_attn, w_in, g_cq, g_ckv, w_uq, w_uk, w_uv, cmp_pe_k, cmp_w1_k, cmp_w2_k, cmp_pe_v, cmp_w1_v, cmp_w2_v, rel_bias, g_grp_mla, g_grp_sb, g_grp_nsa, w_out, g_ffn, w_gate, w_up, w_down, g_ple, w_ple_gate, w_ple, g_final):
    bsz, t, dm = x_prompt.shape
    nb, nq_tok, _ = x_sample.shape
    depth = w_in.shape[0]
    n_pages = page_table.shape[1]
    past = n_pages * cache_mla.shape[2]
    n_buf = state_win.shape[2]
    assert cache_mla.shape[2] == LANE and nq_tok * 8 == ROWS_Q and SEL_BLOCK == 64
    assert past % (PAGES_PER_STEP * LANE) == 0 or n_pages < PAGES_PER_STEP
    tq = min(256, t)
    tm_p = min(512, bsz * t)
    tm_s = min(256, nb * nq_tok)
    win_keep = min(WINDOW, t)
    d = HEAD_DIM

    tab_p = _rope_tables(jnp.arange(t, dtype=jnp.int32))
    tab_s = _rope_tables(jnp.tile(past + jnp.arange(nq_tok, dtype=jnp.int32), tm_s // nq_tok))
    u_tri = jnp.asarray(np.tril(np.ones((SB_CHUNK, SB_CHUNK), np.float32), -1), BF16)
    ptabs = _nsa_prompt_tables(rel_bias, t, tq)
    ov_s, gq, bias_c, bias_w1, bias_w2, tsp, c31_s = _sample_tables(rel_bias, past, nq_tok, n_buf)

    xp = x_prompt.reshape(bsz * t, dm)
    xs = x_sample.reshape(nb * nq_tok, dm)
    outs = [[] for _ in range(8)]
    for i in range(depth):
        final = i == depth - 1
        pw = _proj_weights(w_in[i], w_uq[i], w_uk[i], w_uv[i])
        fw = _finish_weights(g_grp_mla[i], g_grp_sb[i], g_grp_nsa[i], w_out[i], g_ffn[i], w_gate[i], w_up[i],
                             w_down[i], g_ple[i], w_ple_gate[i], w_ple[i], g_final)
        wc, pe = _compress_weights(cmp_pe_k[i], cmp_w1_k[i], cmp_pe_v[i], cmp_w1_v[i])
        w2k, w2v = cmp_w2_k[i].astype(BF16), cmp_w2_v[i].astype(BF16)

        (mla_r, sb_r, nsa_r, win_r, gates_t, sbq_t, nq_t, qa_t, ka, va_t, sb_t, slc_t, win_t) = _proj(
            xp, tab_p, pw, g_attn[i], g_cq[i], g_ckv[i], prompt=True, tm=tm_p, table_period=t, tq=tq)
        o_a = _mla_prompt(qa_t, ka, va_t, b=bsz, t=t, tq=tq)
        o_b = _sb_prompt(sbq_t, sb_r, sb_t, u_tri[:tq, :tq].T, b=bsz, t=t, tq=tq)
        o_c = _nsa_prompt(nq_t, nsa_r, slc_t, win_r, win_t, gates_t, (wc, pe, w2k, w2v), ptabs, b=bsz, t=t, tq=tq)
        xp = _finish(xp, o_a, o_b, o_c, p_prompt[i].reshape(bsz * t, -1), fw, final=final, tm=tm_p)
        outs[0].append(mla_r.reshape(bsz, t, -1))
        outs[2].append(sb_r.reshape(bsz, t, -1))
        outs[4].append(nsa_r.reshape(bsz, t, -1))
        outs[6].append(win_r.reshape(bsz, t, -1)[:, t - win_keep:])

        (mla_n, sb_n, nsa_n, win_n, gates_s, sbq_s, nq_s, qa_s, qlat_s) = _proj(
            xs, tab_s, pw, g_attn[i], g_cq[i], g_ckv[i], prompt=False, tm=tm_s, table_period=tm_s)
        q_pe = qa_s.reshape(nb * nq_tok, H_MLA, SLOT)[:, :, MLA_NOPE:MLA_NOPE + MLA_ROPE]
        qabs = jnp.concatenate([qlat_s.reshape(nb * nq_tok, H_MLA, MLA_KV_RANK), q_pe], axis=2)
        qabs = _rows_q8(qabs.reshape(nb * nq_tok, -1), nb, nq_tok, H_MLA)
        sbq3 = sbq_s.reshape(nb, nq_tok * H_SB, d)
        nq32 = _rows_q8(nq_s, nb, nq_tok, H_NSA)
        consts = (w_uv[i].reshape(MLA_KV_RANK, -1).astype(BF16), wc, pe, w2k, w2v, u_tri, ov_s, gq, bias_c,
                  bias_w1, bias_w2)
        o_a8, o_b4, o_cmp, o_win, idx = _sample_paged(
            i, page_table, qabs, _pad_new(mla_n, nb, nq_tok), sbq3, _pad_new(sb_n, nb, nq_tok), nq32,
            _pad_new(win_n, nb, nq_tok), state_win[i], cache_mla, cache_sb, cache_nsa, consts, past=past)
        gates32 = _rows_q8(gates_s[:, :3 * H_NSA], nb, nq_tok, H_NSA)
        gates32 = jnp.pad(gates32, ((0, 0), (0, 0), (0, LANE - 3)))
        o_c8 = _sample_slc(i, idx[:, :nq_tok, :SEL_TOP_N], page_table, nq32.astype(F32),
                           _pad_new(nsa_n, nb, nq_tok), o_cmp, o_win, gates32, tsp, c31_s, cache_nsa, past=past)
        unrow = lambda a, h: a.reshape(nb, nq_tok, 8, -1)[:, :, :h].reshape(nb * nq_tok, -1)
        xs = _finish(xs, unrow(o_a8, H_MLA), o_b4.reshape(nb * nq_tok, -1), unrow(o_c8, H_NSA),
                     p_sample[i].reshape(nb * nq_tok, -1), fw, final=final, tm=tm_s)
        outs[1].append(mla_n.reshape(nb, nq_tok, -1))
        outs[3].append(sb_n.reshape(nb, nq_tok, -1))
        outs[5].append(nsa_n.reshape(nb, nq_tok, -1))
        win_all = jnp.concatenate([state_win[i], win_n.reshape(nb, nq_tok, -1)], axis=1)
        outs[7].append(win_all[:, nq_tok:])

    y_prompt = xp.reshape(bsz, t, dm)
    y_sample = xs.reshape(nb, nq_tok, dm)
    st = [jnp.stack(o) for o in outs]
    return (y_prompt, y_sample, st[0], st[1], st[2], st[3], st[4], st[5], st[6], st[7])
```
